```python
import math
import jax, jax.numpy as jnp
from jax import lax
import numpy as np

D_MODEL = 2048
BATCH = 2
SEQ = 4096
DEPTH = 1
DEC_BATCH = 32
DEC_SEQ = 8
PAST_LEN = 8192
PAGE_SIZE = 128

DH_A = 128
H_A = (D_MODEL // 2) // DH_A
H_IDX = 16
D_IDX = 64
TOPK_MAX = 256
Q_BLOCK = 128
DK_B = 128
DV_B = 256
H_B = (D_MODEL // 2) // DV_B
GATE_RANK = 16
GATE_TAU = 16.0
GLA_CHUNK = 64
ROPE_THETA = 10000.0
EPS = 1e-6
NEG_BIG = -1e30
MIX_WIDTH = H_A * DH_A + H_B * DV_B
IN_SPLITS = (H_A * DH_A, H_A * DH_A, H_A * DH_A, H_A * DH_A,
             H_IDX * D_IDX, D_IDX, H_IDX,
             H_B * DK_B, H_B * DK_B, H_B * DV_B, H_B * DV_B, GATE_RANK)
IN_WIDTH = sum(IN_SPLITS)

kernel_name = 'hymba_dsa_gla_step'


def rms_norm(x, g):
    xf = x.astype(jnp.float32)
    y = xf * lax.rsqrt(jnp.mean(xf * xf, axis=-1, keepdims=True) + EPS)
    return (y * g.astype(jnp.float32)).astype(x.dtype)


def rope(x, pos):
    half = x.shape[-1] // 2
    inv = ROPE_THETA ** (-jnp.arange(half, dtype=jnp.float32) / half)
    ang = pos.astype(jnp.float32)[:, None] * inv[None, :]
    cos = jnp.cos(ang)[:, None, :]
    sin = jnp.sin(ang)[:, None, :]
    x1 = x[..., :half].astype(jnp.float32)
    x2 = x[..., half:].astype(jnp.float32)
    return jnp.concatenate([x1 * cos - x2 * sin, x2 * cos + x1 * sin], axis=-1).astype(x.dtype)


def project(h, w_in, w_gate_up, b_gate, pos):
    B, T, _ = h.shape
    z = jnp.einsum('btd,de->bte', h, w_in)
    offs = np.cumsum(IN_SPLITS)[:-1].tolist()
    qa, ka, va, ga, qi, ki, wi, qb, kb, vb, gb, ab = jnp.split(z, offs, axis=-1)
    qa = rope(qa.reshape(B, T, H_A, DH_A), pos)
    ka = rope(ka.reshape(B, T, H_A, DH_A), pos)
    va = va.reshape(B, T, H_A, DH_A)
    qi = rope(qi.reshape(B, T, H_IDX, D_IDX), pos)
    ki = rope(ki[:, :, None, :], pos)[:, :, 0, :]
    wi = wi * (H_IDX ** -0.5)
    qb = qb.reshape(B, T, H_B, DK_B) * (DK_B ** -0.5)
    kb = kb.reshape(B, T, H_B, DK_B)
    vb = vb.reshape(B, T, H_B, DV_B)
    log_a = jax.nn.log_sigmoid((jnp.einsum('btr,re->bte', ab, w_gate_up) + b_gate).astype(jnp.float32)) / GATE_TAU
    log_a = log_a.reshape(B, T, H_B, DK_B)
    return qa, ka, va, ga, qi, ki, wi, qb, kb, vb, gb, log_a


def index_scores(qi, wi, ki):
    dots = jnp.einsum('bthd,bsd->btsh', qi.astype(jnp.float32), ki.astype(jnp.float32)) * (D_IDX ** -0.5)
    return jnp.einsum('btsh,bth->bts', jax.nn.relu(dots), wi.astype(jnp.float32))


def sparse_attend(q, k_sel, v_sel, valid):
    s = jnp.einsum('bthd,btkhd->bthk', q.astype(jnp.float32), k_sel.astype(jnp.float32)) * (DH_A ** -0.5)
    s = jnp.where(valid[:, :, None, :], s, NEG_BIG)
    p = jax.nn.softmax(s, axis=-1)
    return jnp.einsum('bthk,btkhd->bthd', p, v_sel.astype(jnp.float32)).astype(q.dtype)


def dsa_prompt(qa, ka, va, qi, ki, wi):
    B, S = qa.shape[:2]
    topk = min(TOPK_MAX, S // 4)
    qblk = min(Q_BLOCK, S)
    n_blk = S // qblk
    b_ix = jnp.arange(B)[:, None, None]
    spos = jnp.arange(S)

    def block(i):
        start = i * qblk
        q_b = lax.dynamic_slice_in_dim(qa, start, qblk, axis=1)
        qi_b = lax.dynamic_slice_in_dim(qi, start, qblk, axis=1)
        wi_b = lax.dynamic_slice_in_dim(wi, start, qblk, axis=1)
        tpos = start + jnp.arange(qblk)
        causal = spos[None, :] <= tpos[:, None]
        sc = jnp.where(causal[None], index_scores(qi_b, wi_b, ki), -jnp.inf)
        _, idx = lax.top_k(sc, topk)
        valid = idx <= tpos[None, :, None]
        k_sel = ka[b_ix, idx]
        v_sel = va[b_ix, idx]
        return sparse_attend(q_b, k_sel, v_sel, valid)

    out = lax.map(block, jnp.arange(n_blk))
    return out.transpose(1, 0, 2, 3, 4).reshape(B, S, H_A * DH_A)


def dsa_sample(qa, ka, va, qi, ki, wi, cache_k, cache_v, cache_idx_k, page_table):
    Bd, T = qa.shape[:2]
    n_pages = page_table.shape[1]
    past = n_pages * PAGE_SIZE
    L = past + T
    topk = min(TOPK_MAX, L // 4)
    phys = (page_table[:, :, None] * PAGE_SIZE + jnp.arange(PAGE_SIZE)[None, None, :]).reshape(Bd, past)
    pool_k = cache_k.reshape(-1, H_A, DH_A)
    pool_v = cache_v.reshape(-1, H_A, DH_A)
    pool_ik = cache_idx_k.reshape(-1, D_IDX)
    ki_all = jnp.concatenate([pool_ik[phys].astype(ki.dtype), ki], axis=1)
    tpos = past + jnp.arange(T)
    causal = jnp.arange(L)[None, :] <= tpos[:, None]
    sc = jnp.where(causal[None], index_scores(qi, wi, ki_all), -jnp.inf)
    _, idx = lax.top_k(sc, topk)
    valid = idx <= tpos[None, :, None]
    in_past = (idx < past)[..., None, None]
    b_ix = jnp.arange(Bd)[:, None, None]
    phys_sel = phys[b_ix, jnp.minimum(idx, past - 1)]
    new_sel = jnp.clip(idx - past, 0, T - 1)
    k_sel = jnp.where(in_past, pool_k[phys_sel].astype(ka.dtype), ka[b_ix, new_sel])
    v_sel = jnp.where(in_past, pool_v[phys_sel].astype(va.dtype), va[b_ix, new_sel])
    return sparse_attend(qa, k_sel, v_sel, valid).reshape(Bd, T, H_A * DH_A)


def gla(q, k, v, log_a, s0):
    B, T = q.shape[:2]
    c = math.gcd(T, GLA_CHUNK)
    n = T // c

    def chunks(x):
        return x.astype(jnp.float32).reshape(B, n, c, H_B, x.shape[-1]).transpose(1, 0, 3, 2, 4)

    causal = jnp.tril(jnp.ones((c, c), dtype=bool))[:, :, None]

    def step(S, inp):
        qt, kt, vt, at = inp
        cum = jnp.cumsum(at, axis=2)
        o_inter = jnp.einsum('bhtk,bhkv->bhtv', qt * jnp.exp(cum), S)
        diff = cum[:, :, :, None, :] - cum[:, :, None, :, :]
        decay = jnp.exp(jnp.where(causal, diff, -jnp.inf))
        att = jnp.einsum('bhtk,bhsk,bhtsk->bhts', qt, kt, decay)
        o = o_inter + jnp.einsum('bhts,bhsv->bhtv', att, vt)
        last = cum[:, :, -1:, :]
        S_new = jnp.exp(last)[:, :, 0, :, None] * S + jnp.einsum('bhsk,bhsv->bhkv', kt * jnp.exp(last - cum), vt)
        return S_new, o

    S_fin, o = lax.scan(step, s0.astype(jnp.float32), (chunks(q), chunks(k), chunks(v), chunks(log_a)))
    o = o.transpose(1, 0, 3, 2, 4).reshape(B, T, H_B, DV_B)
    return o, S_fin


def merge_out(att, ga, o_b, gb, gla_norm, w_out):
    B, T = att.shape[:2]
    a = att.reshape(B, T, -1) * jax.nn.silu(ga)
    bpart = rms_norm(o_b, gla_norm).reshape(B, T, -1).astype(a.dtype) * jax.nn.silu(gb)
    return jnp.einsum('bte,ed->btd', jnp.concatenate([a, bpart], axis=-1), w_out)


def setup_inputs(seed: int = 0) -> dict:
    key = jax.random.key(seed)
    ks = jax.random.split(key, 16)
    n_pages = PAST_LEN // PAGE_SIZE
    used = DEC_BATCH * n_pages
    n_pool = used + used // 4
    f32 = jnp.float32
    x_prompt = jax.random.normal(ks[0], (BATCH, SEQ, D_MODEL), f32)
    x_sample = jax.random.normal(ks[1], (DEC_BATCH, DEC_SEQ, D_MODEL), f32)
    cache_k = jax.random.normal(ks[2], (DEPTH, n_pool, PAGE_SIZE, H_A, DH_A), f32)
    cache_v = jax.random.normal(ks[3], (DEPTH, n_pool, PAGE_SIZE, H_A, DH_A), f32)
    cache_idx_k = jax.random.normal(ks[4], (DEPTH, n_pool, PAGE_SIZE, D_IDX), f32)
    state_gla = jax.random.normal(ks[5], (DEPTH, DEC_BATCH, H_B, DK_B, DV_B), f32)
    page_table = jax.random.permutation(ks[6], n_pool)[:used].reshape(DEC_BATCH, n_pages).astype(jnp.int32)
    norm_in = 1.0 + 0.02 * jax.random.normal(ks[7], (DEPTH, D_MODEL), f32)
    w_in = jax.random.normal(ks[8], (DEPTH, D_MODEL, IN_WIDTH), f32) * (D_MODEL ** -0.5)
    w_gate_up = jax.random.normal(ks[9], (DEPTH, GATE_RANK, H_B * DK_B), f32) * (GATE_RANK ** -0.5)
    b_gate = 0.1 * jax.random.normal(ks[10], (DEPTH, H_B * DK_B), f32)
    gla_norm = 1.0 + 0.02 * jax.random.normal(ks[11], (DEPTH, DV_B), f32)
    w_out = jax.random.normal(ks[12], (DEPTH, MIX_WIDTH, D_MODEL), f32) * (MIX_WIDTH ** -0.5)
    norm_f = 1.0 + 0.02 * jax.random.normal(ks[13], (D_MODEL,), f32)
    return {'x_prompt': x_prompt, 'x_sample': x_sample, 'cache_k': cache_k, 'cache_v': cache_v,
            'cache_idx_k': cache_idx_k, 'state_gla': state_gla, 'page_table': page_table,
            'norm_in': norm_in, 'w_in': w_in, 'w_gate_up': w_gate_up, 'b_gate': b_gate,
            'gla_norm': gla_norm, 'w_out': w_out, 'norm_f': norm_f}


def reference(x_prompt, x_sample, cache_k, cache_v, cache_idx_k, state_gla, page_table,
              norm_in, w_in, w_gate_up, b_gate, gla_norm, w_out, norm_f):
    Bp, Tp = x_prompt.shape[:2]
    Bs, Ts = x_sample.shape[:2]
    past = page_table.shape[1] * PAGE_SIZE
    pos_p = jnp.arange(Tp)
    pos_s = past + jnp.arange(Ts)
    xp, xs = x_prompt, x_sample
    kp_l, vp_l, ikp_l, sp_l, ks_l, vs_l, iks_l, ss_l = [], [], [], [], [], [], [], []
    for l in range(DEPTH):
        hp = rms_norm(xp, norm_in[l])
        qa, ka, va, ga, qi, ki, wi, qb, kb, vb, gb, la = project(hp, w_in[l], w_gate_up[l], b_gate[l], pos_p)
        att = dsa_prompt(qa, ka, va, qi, ki, wi)
        s0 = jnp.zeros((Bp, H_B, DK_B, DV_B), jnp.float32)
        ob, s_fin = gla(qb, kb, vb, la, s0)
        xp = xp + merge_out(att, ga, ob, gb, gla_norm[l], w_out[l])
        kp_l.append(ka); vp_l.append(va); ikp_l.append(ki); sp_l.append(s_fin.astype(state_gla.dtype))
        hs = rms_norm(xs, norm_in[l])
        qa, ka, va, ga, qi, ki, wi, qb, kb, vb, gb, la = project(hs, w_in[l], w_gate_up[l], b_gate[l], pos_s)
        att = dsa_sample(qa, ka, va, qi, ki, wi, cache_k[l], cache_v[l], cache_idx_k[l], page_table)
        ob, s_fin = gla(qb, kb, vb, la, state_gla[l])
        xs = xs + merge_out(att, ga, ob, gb, gla_norm[l], w_out[l])
        ks_l.append(ka); vs_l.append(va); iks_l.append(ki); ss_l.append(s_fin.astype(state_gla.dtype))
    y_prompt = rms_norm(xp, norm_f)
    y_sample = rms_norm(xs, norm_f)
    return (y_prompt, y_sample,
            jnp.stack(kp_l), jnp.stack(vp_l), jnp.stack(ikp_l), jnp.stack(sp_l),
            jnp.stack(ks_l), jnp.stack(vs_l), jnp.stack(iks_l), jnp.stack(ss_l))
```

```python
import functools
import math

import jax
import jax.numpy as jnp
import numpy as np
from jax import lax
from jax.experimental import pallas as pl
from jax.experimental.pallas import tpu as pltpu

F32 = jnp.float32
BF16 = jnp.bfloat16
I32 = jnp.int32

D_MODEL = 2048
PAGE_SIZE = 128
DH_A = 128
H_A = 8
H_IDX = 16
D_IDX = 64
TOPK_MAX = 256
DK_B = 128
DV_B = 256
H_B = 4
GATE_RANK = 16
GATE_TAU = 16.0
ROPE_THETA = 10000.0
EPS = 1e-6
NEG_BIG = -1e30
WA = H_A * DH_A
WI = H_IDX * D_IDX
WKB = H_B * DK_B
WVB = H_B * DV_B

C_QA, C_KA, C_VA, C_GA, C_QI = 0, WA, 2 * WA, 3 * WA, 4 * WA
C_QB = C_QI + WI
C_KB = C_QB + WKB
C_VB = C_KB + WKB
C_GB = C_VB + WVB
N_MAIN = C_GB + WVB
L_KI, L_WI, L_AB = 0, D_IDX, D_IDX + H_IDX
N_TAIL = 128

LANES = 128
INT_MIN = -2 ** 31
VMEM_LIMIT = 48 * 1024 * 1024


def _permute_w_in(w_in):
    o_ki = C_QI + WI
    o_wi = o_ki + D_IDX
    o_qb = o_wi + H_IDX
    o_ab = o_qb + 2 * WKB + 2 * WVB
    main = jnp.concatenate([w_in[:, :o_ki], w_in[:, o_qb:o_ab]], axis=1).astype(BF16)
    pad = jnp.zeros((w_in.shape[0], N_TAIL - D_IDX - H_IDX - GATE_RANK), w_in.dtype)
    tail = jnp.concatenate([w_in[:, o_ki:o_qb], w_in[:, o_ab:], pad], axis=1).astype(BF16)
    return main, tail


def _cparams(n_axes):
    return pltpu.CompilerParams(dimension_semantics=("arbitrary",) * n_axes,
                                vmem_limit_bytes=VMEM_LIMIT)


def _proj_kernel(x_ref, g_ref, w_ref, o_ref, h_ref):
    @pl.when(pl.program_id(1) == 0)
    def _():
        x = x_ref[...]
        ms = jnp.mean(x * x, axis=-1, keepdims=True)
        h_ref[...] = (x * lax.rsqrt(ms + EPS) * g_ref[...]).astype(BF16)

    o_ref[...] = jnp.dot(h_ref[...], w_ref[...], preferred_element_type=F32)


def _proj(x2d, g, w_bf, tm, tn):
    m, k = x2d.shape
    n = w_bf.shape[1]
    return pl.pallas_call(
        _proj_kernel,
        grid=(m // tm, n // tn),
        in_specs=[pl.BlockSpec((tm, k), lambda i, j: (i, 0)),
                  pl.BlockSpec((1, k), lambda i, j: (0, 0)),
                  pl.BlockSpec((k, tn), lambda i, j: (0, j))],
        out_specs=pl.BlockSpec((tm, tn), lambda i, j: (i, j)),
        out_shape=jax.ShapeDtypeStruct((m, n), F32),
        scratch_shapes=[pltpu.VMEM((tm, k), BF16)],
        compiler_params=_cparams(2),
        name="proj",
    )(x2d, g, w_bf)


def _rope128(x, cos, sin_signed):
    return x * cos + pltpu.roll(x, 64, axis=1) * sin_signed


def _rope64(x, cos, sin_lo, sin_hi):
    return x * cos + pltpu.roll(x, 96, axis=1) * sin_lo + pltpu.roll(x, 32, axis=1) * sin_hi


def _rope_kernel(qa_ref, ka_ref, va_ref, qi_ref, tail_ref, tab_ref,
                 newk_ref, newv_ref, newik_ref, qbf_ref, kbf_ref, vbf_ref, qibf_ref,
                 kie_ref, kio_ref):
    cos_a, sin_a = tab_ref[0], tab_ref[1]
    cos_i, sin_lo, sin_hi = tab_ref[2], tab_ref[3], tab_ref[4]
    scale = DH_A ** -0.5
    for h in range(H_A):
        sl = slice(h * DH_A, (h + 1) * DH_A)
        q = _rope128(qa_ref[:, sl], cos_a, sin_a)
        qbf_ref[:, sl] = (q * scale).astype(BF16)
        k = _rope128(ka_ref[:, sl], cos_a, sin_a)
        newk_ref[:, sl] = k
        kbf_ref[:, sl] = k.astype(BF16)
    v = va_ref[...]
    newv_ref[...] = v
    vbf_ref[...] = v.astype(BF16)
    for j in range(WI // LANES):
        sl = slice(j * LANES, (j + 1) * LANES)
        qibf_ref[:, sl] = _rope64(qi_ref[:, sl], cos_i, sin_lo, sin_hi).astype(BF16)
    ki = _rope64(tail_ref[...], cos_i, sin_lo, sin_hi)
    newik_ref[...] = ki[:, L_KI:L_KI + D_IDX]
    lane = lax.broadcasted_iota(I32, ki.shape, 1)
    ki_lo = jnp.where(lane < D_IDX, ki, 0.0)
    kie_ref[...] = ki_lo.astype(BF16)
    kio_ref[...] = pltpu.roll(ki_lo, D_IDX, axis=1).astype(BF16)


def _rope(z, tail, tab, tm):
    m = z.shape[0]
    n_tab = tab.shape[1] // tm
    col = lambda c: pl.BlockSpec((tm, WA), lambda i, c=c: (i, c))
    row = lambda w: pl.BlockSpec((tm, w), lambda i: (i, 0))
    return pl.pallas_call(
        _rope_kernel,
        grid=(m // tm,),
        in_specs=[col(C_QA // WA), col(C_KA // WA), col(C_VA // WA), col(C_QI // WA),
                  row(N_TAIL),
                  pl.BlockSpec((5, tm, LANES), lambda i: (0, i % n_tab, 0))],
        out_specs=[row(WA), row(WA), row(D_IDX), row(WA), row(WA), row(WA), row(WI),
                   row(LANES), row(LANES)],
        out_shape=[jax.ShapeDtypeStruct((m, WA), F32), jax.ShapeDtypeStruct((m, WA), F32),
                   jax.ShapeDtypeStruct((m, D_IDX), F32),
                   jax.ShapeDtypeStruct((m, WA), BF16), jax.ShapeDtypeStruct((m, WA), BF16),
                   jax.ShapeDtypeStruct((m, WA), BF16), jax.ShapeDtypeStruct((m, WI), BF16),
                   jax.ShapeDtypeStruct((m, LANES), BF16), jax.ShapeDtypeStruct((m, LANES), BF16)],
        compiler_params=_cparams(1),
        name="rope",
    )(z, z, z, z, tail, tab)


def _rope_tables(pos):
    pos = pos.astype(F32)[:, None]
    half_a, half_i = DH_A // 2, D_IDX // 2
    inv_a = ROPE_THETA ** (-jnp.arange(half_a, dtype=F32) / half_a)
    inv_i = ROPE_THETA ** (-jnp.arange(half_i, dtype=F32) / half_i)
    ca, sa = jnp.cos(pos * inv_a), jnp.sin(pos * inv_a)
    ci, si = jnp.cos(pos * inv_i), jnp.sin(pos * inv_i)
    zi = jnp.zeros_like(si)
    return jnp.stack([
        jnp.concatenate([ca, ca], axis=1),
        jnp.concatenate([-sa, sa], axis=1),
        jnp.concatenate([ci, ci, ci, ci], axis=1),
        jnp.concatenate([-si, zi, -si, zi], axis=1),
        jnp.concatenate([zi, si, zi, si], axis=1)])


IDX_W_SCALE = (H_IDX ** -0.5) * (D_IDX ** -0.5)


def _idx_scores(q_ref, w, kie, kio):
    nt = (((1,), (1,)), ((), ()))
    acc = jnp.zeros((q_ref.shape[0], kie.shape[0]), F32)
    for j in range(H_IDX // 2):
        qp = q_ref[:, j * LANES:(j + 1) * LANES]
        de = lax.dot_general(qp, kie, nt, preferred_element_type=F32)
        acc = acc + w[:, 2 * j:2 * j + 1] * jnp.maximum(de, 0.0)
        do = lax.dot_general(qp, kio, nt, preferred_element_type=F32)
        acc = acc + w[:, 2 * j + 1:2 * j + 2] * jnp.maximum(do, 0.0)
    return acc


def _idx_prompt_kernel(q_ref, tail_ref, kie_ref, kio_ref, o_ref, *, tq, tc, t_len):
    i = pl.program_id(1)
    w = tail_ref[:, L_WI:L_WI + H_IDX] * IDX_W_SCALE
    n_c = (i * tq + tq - 1) // tc + 1
    row = i * tq + lax.broadcasted_iota(I32, (tq, tc), 0)

    def body(c, carry):
        c0 = pl.multiple_of(c * tc, tc)
        acc = _idx_scores(q_ref, w, kie_ref[pl.ds(c0, tc), :], kio_ref[pl.ds(c0, tc), :])
        col = c0 + lax.broadcasted_iota(I32, (tq, tc), 1)
        o_ref[:, pl.ds(c0, tc)] = jnp.where(col <= row, acc, -jnp.inf)
        return carry

    lax.fori_loop(0, n_c, body, 0)

    def fill(c, carry):
        c0 = pl.multiple_of(c * tc, tc)
        o_ref[:, pl.ds(c0, tc)] = jnp.full((tq, tc), -jnp.inf, F32)
        return carry

    lax.fori_loop(n_c, t_len // tc, fill, 0)


def _idx_prompt(qi_bf, tail, kie, kio, n_batch, t_len, tq=128, tc=256):
    nq = t_len // tq
    kern = functools.partial(_idx_prompt_kernel, tq=tq, tc=tc, t_len=t_len)
    return pl.pallas_call(
        kern,
        grid=(n_batch, nq),
        in_specs=[pl.BlockSpec((tq, WI), lambda b, i: (b * nq + i, 0)),
                  pl.BlockSpec((tq, N_TAIL), lambda b, i: (b * nq + i, 0)),
                  pl.BlockSpec((t_len, LANES), lambda b, i: (b, 0)),
                  pl.BlockSpec((t_len, LANES), lambda b, i: (b, 0))],
        out_specs=pl.BlockSpec((tq, t_len), lambda b, i: (b * nq + i, 0)),
        out_shape=jax.ShapeDtypeStruct((n_batch * t_len, t_len), F32),
        compiler_params=_cparams(2),
        name="idx_prompt",
    )(qi_bf, tail, kie, kio)


KEY_NEG_INF = INT_MIN + 0x7FFFFF


def _select_kernel(s_ref, o_ref, key_ref, *, tr, sub, n_cols, k_top, rows_per_batch):
    tc = sub * LANES
    n_all = n_cols // tc
    if rows_per_batch:
        r0 = (pl.program_id(0) % (rows_per_batch // tr)) * tr
        n_c = (r0 + tr - 1) // tc + 1
    else:
        n_c = n_all

    def to_keys(c, carry):
        c0 = pl.multiple_of(c * tc, tc)
        b = pltpu.bitcast(s_ref[:, pl.ds(c0, tc)], I32)
        key_ref[:, pl.ds(c0, tc)] = b ^ ((b >> 31) & 0x7FFFFFFF)
        return carry

    lax.fori_loop(0, n_c, to_keys, 0)

    def count_ge(cand):
        def body(c, acc):
            c0 = pl.multiple_of(c * tc, tc)
            for u in range(sub):
                kc = key_ref[:, pl.ds(c0 + u * LANES, LANES)]
                acc = acc + jnp.where(kc >= cand, 1.0, 0.0)
            return acc

        acc = lax.fori_loop(0, n_c, body, jnp.zeros((tr, LANES), F32))
        return jnp.sum(acc, axis=1, keepdims=True)

    kf = float(k_top)
    t0 = jnp.where(count_ge(jnp.zeros((tr, 1), I32)) >= kf, 0, INT_MIN).astype(I32)

    def bit_step(j, t):
        cand = t | jnp.left_shift(jnp.int32(1), 30 - j)
        return jnp.where(count_ge(cand) >= kf, cand, t)

    thr = lax.fori_loop(0, 31, bit_step, t0)
    thr = jnp.maximum(thr, KEY_NEG_INF + 1)

    def emit(c, carry):
        c0 = pl.multiple_of(c * tc, tc)
        kc = key_ref[:, pl.ds(c0, tc)]
        o_ref[:, pl.ds(c0, tc)] = jnp.where(kc >= thr, 0.0, NEG_BIG).astype(o_ref.dtype)
        return carry

    lax.fori_loop(0, n_c, emit, 0)

    def fill(c, carry):
        c0 = pl.multiple_of(c * tc, tc)
        o_ref[:, pl.ds(c0, tc)] = jnp.full((tr, tc), NEG_BIG, o_ref.dtype)
        return carry

    lax.fori_loop(n_c, n_all, fill, 0)


def _select(scores, k_top, sub, rows_per_batch, out_dtype, tr=128):
    m, n_cols = scores.shape
    kern = functools.partial(_select_kernel, tr=tr, sub=sub, n_cols=n_cols, k_top=k_top,
                             rows_per_batch=rows_per_batch)
    return pl.pallas_call(
        kern,
        grid=(m // tr,),
        in_specs=[pl.BlockSpec((tr, n_cols), lambda i: (i, 0))],
        out_specs=pl.BlockSpec((tr, n_cols), lambda i: (i, 0)),
        out_shape=jax.ShapeDtypeStruct((m, n_cols), out_dtype),
        scratch_shapes=[pltpu.VMEM((tr, n_cols), I32)],
        compiler_params=_cparams(1),
        name="select",
    )(scores)


def _attn_prompt_kernel(q_ref, k_ref, v_ref, b_ref, o_ref, m_ref, l_ref, acc_ref, *, tq, ts):
    i, j = pl.program_id(1), pl.program_id(2)
    n_rep = ts // LANES

    @pl.when(j == 0)
    def _():
        m_ref[...] = jnp.full(m_ref.shape, NEG_BIG, F32)
        l_ref[...] = jnp.zeros(l_ref.shape, F32)
        acc_ref[...] = jnp.zeros(acc_ref.shape, F32)

    @pl.when(j * ts <= i * tq + tq - 1)
    def _():
        bias = b_ref[...].astype(F32)
        nt = (((1,), (1,)), ((), ()))
        for h in range(H_A):
            sl = slice(h * DH_A, (h + 1) * DH_A)
            s = lax.dot_general(q_ref[:, sl], k_ref[:, sl], nt, preferred_element_type=F32) + bias
            m_prev = m_ref[h]
            m_next = jnp.maximum(m_prev, jnp.max(s, axis=1, keepdims=True))
            alpha = jnp.exp(m_prev - m_next)
            p = jnp.exp(s - jnp.tile(m_next, (1, n_rep)))
            l_ref[h] = alpha * l_ref[h] + jnp.sum(p, axis=1, keepdims=True)
            m_ref[h] = m_next
            pv = jnp.dot(p.astype(BF16), v_ref[:, sl], preferred_element_type=F32)
            acc_ref[:, sl] = acc_ref[:, sl] * alpha + pv

    @pl.when(j == pl.num_programs(2) - 1)
    def _():
        for h in range(H_A):
            sl = slice(h * DH_A, (h + 1) * DH_A)
            o_ref[:, sl] = (acc_ref[:, sl] / l_ref[h]).astype(o_ref.dtype)


def _attn_prompt(q_bf, k_bf, v_bf, bias, n_batch, t_len, tq=256, ts=512):
    nq, ns = t_len // tq, t_len // ts
    last = lambda i: (i * tq + tq - 1) // ts
    kern = functools.partial(_attn_prompt_kernel, tq=tq, ts=ts)
    return pl.pallas_call(
        kern,
        grid=(n_batch, nq, ns),
        in_specs=[pl.BlockSpec((tq, WA), lambda b, i, j: (b * nq + i, 0)),
                  pl.BlockSpec((ts, WA), lambda b, i, j: (b * ns + jnp.minimum(j, last(i)), 0)),
                  pl.BlockSpec((ts, WA), lambda b, i, j: (b * ns + jnp.minimum(j, last(i)), 0)),
                  pl.BlockSpec((tq, ts), lambda b, i, j: (b * nq + i, jnp.minimum(j, last(i))))],
        out_specs=pl.BlockSpec((tq, WA), lambda b, i, j: (b * nq + i, 0)),
        out_shape=jax.ShapeDtypeStruct((n_batch * t_len, WA), F32),
        scratch_shapes=[pltpu.VMEM((H_A, tq, LANES), F32), pltpu.VMEM((H_A, tq, LANES), F32),
                        pltpu.VMEM((tq, WA), F32)],
        compiler_params=_cparams(3),
        name="attn_prompt",
    )(q_bf, k_bf, v_bf, bias)


def _log_sigmoid(x):
    return jnp.minimum(x, 0.0) - jnp.log1p(jnp.exp(-jnp.abs(x)))


def _bf16_split3(x):
    x1 = x.astype(BF16)
    r1 = x - x1.astype(F32)
    x2 = r1.astype(BF16)
    x3 = (r1 - x2.astype(F32)).astype(BF16)
    return x1, x2, x3


def _gla_kernel(q_ref, k_ref, v_ref, tail_ref, wg_ref, bg_ref, s0_ref, o_ref, sfin_ref,
                st_ref, cum_ref, kk_ref, xs_ref, *, tb, chunk, sub):
    c_id = pl.program_id(2)
    n_sub = chunk // sub
    nt = (((1,), (1,)), ((), ()))
    tn = (((0,), (0,)), ((), ()))

    @pl.when(c_id == 0)
    def _():
        st_ref[...] = s0_ref[...].T

    rows = lax.broadcasted_iota(I32, (chunk, LANES), 0)
    cols = lax.broadcasted_iota(I32, (chunk, LANES), 1)
    tri = (lax.broadcasted_iota(I32, (chunk, chunk), 1)
           <= lax.broadcasted_iota(I32, (chunk, chunk), 0)).astype(BF16)
    ones = jnp.ones((LANES, LANES), BF16)
    wg = wg_ref[...].astype(BF16)
    bg = bg_ref[...]

    def load(ref, r0, n_rows):
        x = ref[pl.ds(r0, n_rows), :]
        if n_rows < chunk:
            x = jnp.concatenate([x, jnp.zeros((chunk - n_rows, x.shape[1]), x.dtype)], axis=0)
        return x

    def do_chunk(r0, n_rows):
        q = load(q_ref, r0, n_rows) * (DK_B ** -0.5)
        k = load(k_ref, r0, n_rows)
        v = load(v_ref, r0, n_rows)
        ab = load(tail_ref, r0, n_rows)[:, L_AB:L_AB + GATE_RANK].astype(BF16)
        la = _log_sigmoid(jnp.dot(ab, wg, preferred_element_type=F32) + bg) / GATE_TAU
        if n_rows < chunk:
            la = jnp.where(rows < n_rows, la, 0.0)
        l1, l2, l3 = _bf16_split3(la)
        cum = (jnp.dot(tri, l1, preferred_element_type=F32)
               + jnp.dot(tri, l2, preferred_element_type=F32)
               + jnp.dot(tri, l3, preferred_element_type=F32))
        cum_ref[...] = cum
        kk_ref[...] = k
        st = st_ref[...]
        vb = v.astype(BF16)

        o = lax.dot_general((q * jnp.exp(cum)).astype(BF16), st.astype(BF16), nt,
                            preferred_element_type=F32)

        att = jnp.zeros((chunk, LANES), F32)
        for i in range(1, n_sub):
            r_i = cum_ref[i * sub - 1:i * sub, :]
            qt = (q * jnp.exp(jnp.minimum(cum - r_i, 0.0))).astype(BF16)
            kt = (k * jnp.exp(jnp.minimum(r_i - cum, 0.0))).astype(BF16)
            a_i = lax.dot_general(qt, kt, nt, preferred_element_type=F32)
            if chunk < LANES:
                a_i = jnp.concatenate([a_i, jnp.zeros((chunk, LANES - chunk), F32)], axis=1)
            att = jnp.where((rows // sub == i) & (cols < i * sub), a_i, att)

        for sl in range(sub):
            cum_s = jnp.concatenate(
                [jnp.broadcast_to(cum_ref[i * sub + sl:i * sub + sl + 1, :], (sub, LANES))
                 for i in range(n_sub)], axis=0)
            k_s = jnp.concatenate(
                [jnp.broadcast_to(kk_ref[i * sub + sl:i * sub + sl + 1, :], (sub, LANES))
                 for i in range(n_sub)], axis=0)
            dec = jnp.where(rows % sub >= sl, jnp.exp(jnp.minimum(cum - cum_s, 0.0)), 0.0)
            xs_ref[sl * chunk:(sl + 1) * chunk, :] = (q * k_s * dec).astype(BF16)
        red = jnp.dot(xs_ref[...], ones, preferred_element_type=F32)
        for sl in range(sub):
            hit = (cols % sub == sl) & (cols // sub == rows // sub) & (cols < chunk)
            att = jnp.where(hit, red[sl * chunk:(sl + 1) * chunk, :], att)

        o = o + jnp.dot(att[:, :chunk].astype(BF16), vb, preferred_element_type=F32)
        o_ref[pl.ds(r0, n_rows), :] = o[:n_rows]

        last = cum_ref[chunk - 1:chunk, :]
        kd = (k * jnp.exp(last - cum)).astype(BF16)
        st_ref[...] = st * jnp.exp(last) + lax.dot_general(vb, kd, tn, preferred_element_type=F32)

    if tb < chunk:
        do_chunk(0, tb)
    else:
        def body(c, carry):
            do_chunk(pl.multiple_of(c * chunk, chunk), chunk)
            return carry
        lax.fori_loop(0, tb // chunk, body, 0)

    @pl.when(c_id == pl.num_programs(2) - 1)
    def _():
        sfin_ref[...] = st_ref[...].T


def _gla(z, tail, w_gate_up, b_gate, s0, n_batch, t_len, tb, chunk, sub):
    nb = t_len // tb
    kern = functools.partial(_gla_kernel, tb=tb, chunk=chunk, sub=sub)
    return pl.pallas_call(
        kern,
        grid=(n_batch, H_B, nb),
        in_specs=[pl.BlockSpec((tb, DK_B), lambda b, h, c: (b * nb + c, C_QB // DK_B + h)),
                  pl.BlockSpec((tb, DK_B), lambda b, h, c: (b * nb + c, C_KB // DK_B + h)),
                  pl.BlockSpec((tb, DV_B), lambda b, h, c: (b * nb + c, C_VB // DV_B + h)),
                  pl.BlockSpec((tb, N_TAIL), lambda b, h, c: (b * nb + c, 0)),
                  pl.BlockSpec((GATE_RANK, DK_B), lambda b, h, c: (0, h)),
                  pl.BlockSpec((1, DK_B), lambda b, h, c: (0, h)),
                  pl.BlockSpec((None, None, DK_B, DV_B), lambda b, h, c: (b, h, 0, 0))],
        out_specs=[pl.BlockSpec((tb, DV_B), lambda b, h, c: (b * nb + c, h)),
                   pl.BlockSpec((None, None, DK_B, DV_B), lambda b, h, c: (b, h, 0, 0))],
        out_shape=[jax.ShapeDtypeStruct((n_batch * t_len, WVB), F32),
                   jax.ShapeDtypeStruct((n_batch, H_B, DK_B, DV_B), F32)],
        scratch_shapes=[pltpu.VMEM((DV_B, DK_B), F32), pltpu.VMEM((chunk, LANES), F32),
                        pltpu.VMEM((chunk, LANES), F32), pltpu.VMEM((sub * chunk, LANES), BF16)],
        compiler_params=_cparams(3),
        name="gla",
    )(z, z, z, tail, w_gate_up, b_gate, s0)


def _silu(x):
    return x / (1.0 + jnp.exp(-x))


def _merge_kernel(att_ref, ga_ref, ob_ref, gb_ref, gn_ref, wo_ref, x_ref, nf_ref, o_ref):
    a = (att_ref[...] * _silu(ga_ref[...])).astype(BF16)
    y = jnp.dot(a, wo_ref[:WA, :], preferred_element_type=F32)
    gn = gn_ref[...]
    for h in range(H_B):
        sl = slice(h * DV_B, (h + 1) * DV_B)
        ob = ob_ref[:, sl]
        bn = ob * lax.rsqrt(jnp.mean(ob * ob, axis=-1, keepdims=True) + EPS) * gn
        bp = (bn * _silu(gb_ref[:, sl])).astype(BF16)
        y = y + jnp.dot(bp, wo_ref[WA + h * DV_B:WA + (h + 1) * DV_B, :], preferred_element_type=F32)
    xo = x_ref[...] + y
    o_ref[...] = xo * lax.rsqrt(jnp.mean(xo * xo, axis=-1, keepdims=True) + EPS) * nf_ref[...]


def _merge(att, z, ob, gla_norm, w_out_bf, x2d, norm_f, tm):
    m = x2d.shape[0]
    return pl.pallas_call(
        _merge_kernel,
        grid=(m // tm,),
        in_specs=[pl.BlockSpec((tm, WA), lambda i: (i, 0)),
                  pl.BlockSpec((tm, WA), lambda i: (i, C_GA // WA)),
                  pl.BlockSpec((tm, WVB), lambda i: (i, 0)),
                  pl.BlockSpec((tm, WVB), lambda i: (i, C_GB // WVB)),
                  pl.BlockSpec((1, DV_B), lambda i: (0, 0)),
                  pl.BlockSpec((WA + WVB, D_MODEL), lambda i: (0, 0)),
                  pl.BlockSpec((tm, D_MODEL), lambda i: (i, 0)),
                  pl.BlockSpec((1, D_MODEL), lambda i: (0, 0))],
        out_specs=pl.BlockSpec((tm, D_MODEL), lambda i: (i, 0)),
        out_shape=jax.ShapeDtypeStruct((m, D_MODEL), F32),
        compiler_params=_cparams(1),
        name="merge",
    )(att, z, ob, z, gla_norm, w_out_bf, x2d, norm_f)


PAGES_PER_STEP = 8


def _idx_pages_kernel(pt_ref, q_ref, w_ref, *refs, n_pages, t_dec, causal):
    page_refs, o_ref = refs[:n_pages], refs[n_pages]
    nt = (((1,), (1,)), ((), ()))
    q = q_ref[...]
    w = w_ref[...]
    for p in range(n_pages):
        kp = page_refs[p][...].astype(BF16)
        d = lax.dot_general(q, kp, nt, preferred_element_type=F32)
        r = w * jnp.maximum(d, 0.0)
        acc = r[0:t_dec]
        for h in range(1, H_IDX):
            acc = acc + r[h * t_dec:(h + 1) * t_dec]
        if causal:
            t_i = lax.broadcasted_iota(I32, acc.shape, 0)
            s_i = lax.broadcasted_iota(I32, acc.shape, 1)
            acc = jnp.where(s_i <= t_i, acc, -jnp.inf)
        o_ref[:, p * PAGE_SIZE:(p + 1) * PAGE_SIZE] = acc


def _idx_sample_past(page_table, q_stk, w_stk, pool_ik, n_seq, t_dec, n_pages):
    pps = PAGES_PER_STEP
    n_g = n_pages // pps
    rows = H_IDX * t_dec
    kern = functools.partial(_idx_pages_kernel, n_pages=pps, t_dec=t_dec, causal=False)
    page_spec = lambda r: pl.BlockSpec(
        (None, PAGE_SIZE, D_IDX), lambda b, g, pt, r=r: (pt[b * n_pages + g * pps + r], 0, 0))
    return pl.pallas_call(
        kern,
        grid_spec=pltpu.PrefetchScalarGridSpec(
            num_scalar_prefetch=1,
            grid=(n_seq, n_g),
            in_specs=[pl.BlockSpec((rows, D_IDX), lambda b, g, pt: (b, 0)),
                      pl.BlockSpec((rows, 1), lambda b, g, pt: (b, 0))]
                     + [page_spec(r) for r in range(pps)],
            out_specs=pl.BlockSpec((t_dec, pps * PAGE_SIZE), lambda b, g, pt: (b, g))),
        out_shape=jax.ShapeDtypeStruct((n_seq * t_dec, n_pages * PAGE_SIZE), F32),
        compiler_params=_cparams(2),
        name="idx_sample_past",
    )(page_table.reshape(-1), q_stk, w_stk, *([pool_ik] * pps))


def _idx_sample_new(q_stk, w_stk, ki_new_pad, n_seq, t_dec):
    rows = H_IDX * t_dec
    kern = functools.partial(_idx_pages_kernel, None, n_pages=1, t_dec=t_dec, causal=True)
    return pl.pallas_call(
        kern,
        grid=(n_seq,),
        in_specs=[pl.BlockSpec((rows, D_IDX), lambda b: (b, 0)),
                  pl.BlockSpec((rows, 1), lambda b: (b, 0)),
                  pl.BlockSpec((None, PAGE_SIZE, D_IDX), lambda b: (b, 0, 0))],
        out_specs=pl.BlockSpec((t_dec, PAGE_SIZE), lambda b: (b, 0)),
        out_shape=jax.ShapeDtypeStruct((n_seq * t_dec, PAGE_SIZE), F32),
        compiler_params=_cparams(1),
        name="idx_sample_new",
    )(q_stk, w_stk, ki_new_pad)


ATTN_PAGES_PER_STEP = 4


def _attn_sample_kernel(pt_ref, q_ref, sel_ref, selnew_ref, e_ref, knew_ref, vnew_ref, *refs,
                        n_pages, t_dec):
    k_refs, v_refs = refs[:n_pages], refs[n_pages:2 * n_pages]
    o_ref, m_ref, l_ref, acc_ref = refs[2 * n_pages:]
    g = pl.program_id(1)
    rows = H_A * t_dec
    nt = (((1,), (1,)), ((), ()))
    q = q_ref[...]
    e = e_ref[...]
    n_col = PAGE_SIZE * H_A
    head_ok = (lax.broadcasted_iota(I32, (rows, n_col), 0) // t_dec
               == lax.broadcasted_iota(I32, (rows, n_col), 1) % H_A)

    def update(sel, kp, vp):
        picked = jnp.where(jnp.tile(sel, (H_A, 1)) == 0.0, 1.0, 0.0).astype(BF16)
        picked = jnp.dot(picked, e, preferred_element_type=F32)
        s = lax.dot_general(q, kp.astype(BF16), nt, preferred_element_type=F32)
        s = jnp.where(head_ok, jnp.where(picked > 0.5, s, NEG_BIG), NEG_BIG)
        m_prev = m_ref[...]
        m_next = jnp.maximum(m_prev, jnp.max(s, axis=1, keepdims=True))
        alpha = jnp.exp(m_prev - m_next)
        p = jnp.exp(s - jnp.tile(m_next, (1, n_col // LANES)))
        l_ref[...] = alpha * l_ref[...] + jnp.sum(p, axis=1, keepdims=True)
        m_ref[...] = m_next
        acc_ref[...] = acc_ref[...] * alpha + jnp.dot(p.astype(BF16), vp.astype(BF16),
                                                      preferred_element_type=F32)

    @pl.when(g == 0)
    def _():
        m_ref[...] = jnp.full(m_ref.shape, NEG_BIG, F32)
        l_ref[...] = jnp.zeros(l_ref.shape, F32)
        acc_ref[...] = jnp.zeros(acc_ref.shape, F32)
        update(selnew_ref[...], knew_ref[...], vnew_ref[...])

    for p_i in range(n_pages):
        update(sel_ref[:, p_i * PAGE_SIZE:(p_i + 1) * PAGE_SIZE], k_refs[p_i][...], v_refs[p_i][...])

    @pl.when(g == pl.num_programs(1) - 1)
    def _():
        o_ref[...] = acc_ref[...] / l_ref[...]


def _attn_sample(page_table, q_stk, sel_past, sel_new, expand, knew_pad, vnew_pad, pool_k, pool_v,
                 n_seq, t_dec, n_pages):
    pps = ATTN_PAGES_PER_STEP
    n_g = n_pages // pps
    rows = H_A * t_dec
    n_col = PAGE_SIZE * H_A
    kern = functools.partial(_attn_sample_kernel, n_pages=pps, t_dec=t_dec)
    page_spec = lambda r: pl.BlockSpec(
        (None, n_col, DH_A), lambda b, g, pt, r=r: (pt[b * n_pages + g * pps + r], 0, 0))
    return pl.pallas_call(
        kern,
        grid_spec=pltpu.PrefetchScalarGridSpec(
            num_scalar_prefetch=1,
            grid=(n_seq, n_g),
            in_specs=[pl.BlockSpec((rows, DH_A), lambda b, g, pt: (b, 0)),
                      pl.BlockSpec((t_dec, pps * PAGE_SIZE), lambda b, g, pt: (b, g)),
                      pl.BlockSpec((t_dec, PAGE_SIZE), lambda b, g, pt: (b, 0)),
                      pl.BlockSpec((PAGE_SIZE, n_col), lambda b, g, pt: (0, 0)),
                      pl.BlockSpec((None, n_col, DH_A), lambda b, g, pt: (b, 0, 0)),
                      pl.BlockSpec((None, n_col, DH_A), lambda b, g, pt: (b, 0, 0))]
                     + [page_spec(r) for r in range(pps)] * 2,
            out_specs=pl.BlockSpec((rows, DH_A), lambda b, g, pt: (b, 0)),
            scratch_shapes=[pltpu.VMEM((rows, LANES), F32), pltpu.VMEM((rows, LANES), F32),
                            pltpu.VMEM((rows, DH_A), F32)]),
        out_shape=jax.ShapeDtypeStruct((n_seq * rows, DH_A), F32),
        compiler_params=_cparams(2),
        name="attn_sample",
    )(page_table.reshape(-1), q_stk, sel_past, sel_new, expand, knew_pad, vnew_pad,
      *([pool_k] * pps), *([pool_v] * pps))


def _group_common(x, norm_in, w_main, w_tail, pos, tm, tn, tr):
    n_seq, t_len, _ = x.shape
    x2d = x.reshape(n_seq * t_len, D_MODEL)
    z = _proj(x2d, norm_in, w_main, tm, tn)
    tail = _proj(x2d, norm_in, w_tail, tm, N_TAIL)
    tab = _rope_tables(pos)
    return x2d, z, tail, _rope(z, tail, tab, tr)


def kernel(x_prompt, x_sample, cache_k, cache_v, cache_idx_k, state_gla, page_table,
           norm_in, w_in, w_gate_up, b_gate, gla_norm, w_out, norm_f):
    n_p, t_p, _ = x_prompt.shape
    n_s, t_s, _ = x_sample.shape
    n_pages = page_table.shape[1]
    past = n_pages * PAGE_SIZE
    assert cache_k.shape[0] == 1, "single layer"

    w_main, w_tail = _permute_w_in(w_in[0])
    w_out_bf = w_out[0].astype(BF16)
    g_in, g_gla, g_f = norm_in[0][None], gla_norm[0][None], norm_f[None]
    wg, bg = w_gate_up[0], b_gate[0][None]

    xp2d, zp, tailp, (newk_p, newv_p, newik_p, q_p, k_p, v_p, qi_p, kie_p, kio_p) = _group_common(
        x_prompt, g_in, w_main, w_tail, jnp.arange(t_p), 1024, 512, 256)
    sc_p = _idx_prompt(qi_p, tailp, kie_p, kio_p, n_p, t_p)
    bias_p = _select(sc_p, min(TOPK_MAX, t_p // 4), 4, t_p, BF16)
    att_p = _attn_prompt(q_p, k_p, v_p, bias_p, n_p, t_p)
    s0_p = jnp.zeros((n_p, H_B, DK_B, DV_B), F32)
    ob_p, sfin_p = _gla(zp, tailp, wg, bg, s0_p, n_p, t_p, 256, 64, 16)
    y_p = _merge(att_p, zp, ob_p, g_gla, w_out_bf, xp2d, g_f, 256)

    m_s = n_s * t_s
    pos_s = jnp.tile(past + jnp.arange(t_s), n_s)
    xs2d, zs, tails, (newk_s, newv_s, newik_s, q_s, k_s, v_s, qi_s, kie_s, kio_s) = _group_common(
        x_sample, g_in, w_main, w_tail, pos_s, m_s, 512, m_s)
    qi_stk = qi_s.reshape(n_s, t_s, H_IDX, D_IDX).transpose(0, 2, 1, 3).reshape(n_s * H_IDX * t_s, D_IDX)
    w_stk = (tails[:, L_WI:L_WI + H_IDX] * IDX_W_SCALE).reshape(n_s, t_s, H_IDX).transpose(0, 2, 1)
    w_stk = w_stk.reshape(n_s * H_IDX * t_s, 1)
    q_stk = q_s.reshape(n_s, t_s, H_A, DH_A).transpose(0, 2, 1, 3).reshape(n_s * H_A * t_s, DH_A)
    pad_rows = lambda a, n: jnp.pad(a, ((0, 0), (0, n - a.shape[1]), (0, 0)))
    ki_new_pad = pad_rows(newik_s.reshape(n_s, t_s, D_IDX), PAGE_SIZE)
    knew_pad = pad_rows(newk_s.reshape(n_s, t_s * H_A, DH_A), PAGE_SIZE * H_A)
    vnew_pad = pad_rows(newv_s.reshape(n_s, t_s * H_A, DH_A), PAGE_SIZE * H_A)
    pool_ik = cache_idx_k[0]
    pool_k = cache_k[0].reshape(-1, PAGE_SIZE * H_A, DH_A)
    pool_v = cache_v[0].reshape(-1, PAGE_SIZE * H_A, DH_A)

    sc_past = _idx_sample_past(page_table, qi_stk, w_stk, pool_ik, n_s, t_s, n_pages)
    sc_new = _idx_sample_new(qi_stk, w_stk, ki_new_pad, n_s, t_s)
    sc_s = jnp.concatenate([sc_past, sc_new], axis=1)
    k_top_s = min(TOPK_MAX, (past + t_s) // 4)
    n_tiles_s = sc_s.shape[1] // LANES
    sub_s = max(d for d in range(1, 6) if n_tiles_s % d == 0)
    bias_s = _select(sc_s, k_top_s, sub_s, 0, F32)
    expand = (jnp.arange(PAGE_SIZE * H_A)[None, :] // H_A == jnp.arange(PAGE_SIZE)[:, None]).astype(BF16)
    att_stk = _attn_sample(page_table, q_stk, bias_s[:, :past], bias_s[:, past:], expand,
                           knew_pad, vnew_pad, pool_k, pool_v, n_s, t_s, n_pages)
    att_s = att_stk.reshape(n_s, H_A, t_s, DH_A).transpose(0, 2, 1, 3).reshape(m_s, WA)
    ob_s, sfin_s = _gla(zs, tails, wg, bg, state_gla[0], n_s, t_s, t_s, 16, 16)
    y_s = _merge(att_s, zs, ob_s, g_gla, w_out_bf, xs2d, g_f, m_s)

    return (y_p.reshape(n_p, t_p, D_MODEL), y_s.reshape(n_s, t_s, D_MODEL),
            newk_p.reshape(1, n_p, t_p, H_A, DH_A), newv_p.reshape(1, n_p, t_p, H_A, DH_A),
            newik_p.reshape(1, n_p, t_p, D_IDX), sfin_p[None],
            newk_s.reshape(1, n_s, t_s, H_A, DH_A), newv_s.reshape(1, n_s, t_s, H_A, DH_A),
            newik_s.reshape(1, n_s, t_s, D_IDX), sfin_s[None])
```

```python
import functools
import math

import jax
import jax.numpy as jnp
import numpy as np
from jax import lax
from jax.experimental import pallas as pl
from jax.experimental.pallas import tpu as pltpu

F32 = jnp.float32
BF16 = jnp.bfloat16
I32 = jnp.int32

D_MODEL = 2048
PAGE_SIZE = 128
DH_A = 128
H_A = 8
H_IDX = 16
D_IDX = 64
TOPK_MAX = 256
DK_B = 128
DV_B = 256
H_B = 4
GATE_RANK = 16
GATE_TAU = 16.0
ROPE_THETA = 10000.0
EPS = 1e-6
NEG_BIG = -1e30
WA = H_A * DH_A
WI = H_IDX * D_IDX
WKB = H_B * DK_B
WVB = H_B * DV_B

C_QA, C_KA, C_VA, C_GA, C_QI = 0, WA, 2 * WA, 3 * WA, 4 * WA
C_QB = C_QI + WI
C_KB = C_QB + WKB
C_VB = C_KB + WKB
C_GB = C_VB + WVB
N_MAIN = C_GB + WVB
L_KI, L_WI, L_AB = 0, D_IDX, D_IDX + H_IDX
N_TAIL = 128

LANES = 128
INT_MIN = -2 ** 31
VMEM_LIMIT = 48 * 1024 * 1024


def _permute_w_in(w_in):
    o_ki = C_QI + WI
    o_wi = o_ki + D_IDX
    o_qb = o_wi + H_IDX
    o_ab = o_qb + 2 * WKB + 2 * WVB
    main = jnp.concatenate([w_in[:, :o_ki], w_in[:, o_qb:o_ab]], axis=1).astype(BF16)
    pad = jnp.zeros((w_in.shape[0], N_TAIL - D_IDX - H_IDX - GATE_RANK), w_in.dtype)
    tail = jnp.concatenate([w_in[:, o_ki:o_qb], w_in[:, o_ab:], pad], axis=1).astype(BF16)
    return main, tail


def _cparams(n_axes):
    return pltpu.CompilerParams(dimension_semantics=("arbitrary",) * n_axes,
                                vmem_limit_bytes=VMEM_LIMIT)


def _proj_kernel(x_ref, g_ref, w_ref, o_ref, h_ref):
    @pl.when(pl.program_id(1) == 0)
    def _():
        x = x_ref[...]
        ms = jnp.mean(x * x, axis=-1, keepdims=True)
        h_ref[...] = (x * lax.rsqrt(ms + EPS) * g_ref[...]).astype(BF16)

    o_ref[...] = jnp.dot(h_ref[...], w_ref[...], preferred_element_type=F32)


def _proj(x2d, g, w_bf, tm, tn):
    m, k = x2d.shape
    n = w_bf.shape[1]
    return pl.pallas_call(
        _proj_kernel,
        grid=(m // tm, n // tn),
        in_specs=[pl.BlockSpec((tm, k), lambda i, j: (i, 0)),
                  pl.BlockSpec((1, k), lambda i, j: (0, 0)),
                  pl.BlockSpec((k, tn), lambda i, j: (0, j))],
        out_specs=pl.BlockSpec((tm, tn), lambda i, j: (i, j)),
        out_shape=jax.ShapeDtypeStruct((m, n), F32),
        scratch_shapes=[pltpu.VMEM((tm, k), BF16)],
        compiler_params=_cparams(2),
        name="proj",
    )(x2d, g, w_bf)


def _rope128(x, cos, sin_signed):
    return x * cos + pltpu.roll(x, 64, axis=1) * sin_signed


def _rope64(x, cos, sin_lo, sin_hi):
    return x * cos + pltpu.roll(x, 96, axis=1) * sin_lo + pltpu.roll(x, 32, axis=1) * sin_hi


def _rope_kernel(qa_ref, ka_ref, va_ref, qi_ref, tail_ref, tab_ref,
                 newk_ref, newv_ref, newik_ref, qbf_ref, kbf_ref, vbf_ref, qibf_ref,
                 kie_ref, kio_ref):
    cos_a, sin_a = tab_ref[0], tab_ref[1]
    cos_i, sin_lo, sin_hi = tab_ref[2], tab_ref[3], tab_ref[4]
    scale = DH_A ** -0.5
    for h in range(H_A):
        sl = slice(h * DH_A, (h + 1) * DH_A)
        q = _rope128(qa_ref[:, sl], cos_a, sin_a)
        qbf_ref[:, sl] = (q * scale).astype(BF16)
        k = _rope128(ka_ref[:, sl], cos_a, sin_a)
        newk_ref[:, sl] = k
        kbf_ref[:, sl] = k.astype(BF16)
    v = va_ref[...]
    newv_ref[...] = v
    vbf_ref[...] = v.astype(BF16)
    for j in range(WI // LANES):
        sl = slice(j * LANES, (j + 1) * LANES)
        qibf_ref[:, sl] = _rope64(qi_ref[:, sl], cos_i, sin_lo, sin_hi).astype(BF16)
    ki = _rope64(tail_ref[...], cos_i, sin_lo, sin_hi)
    newik_ref[...] = ki[:, L_KI:L_KI + D_IDX]
    lane = lax.broadcasted_iota(I32, ki.shape, 1)
    ki_lo = jnp.where(lane < D_IDX, ki, 0.0)
    kie_ref[...] = ki_lo.astype(BF16)
    kio_ref[...] = pltpu.roll(ki_lo, D_IDX, axis=1).astype(BF16)


def _rope(z, tail, tab, tm):
    m = z.shape[0]
    n_tab = tab.shape[1] // tm
    col = lambda c: pl.BlockSpec((tm, WA), lambda i, c=c: (i, c))
    row = lambda w: pl.BlockSpec((tm, w), lambda i: (i, 0))
    return pl.pallas_call(
        _rope_kernel,
        grid=(m // tm,),
        in_specs=[col(C_QA // WA), col(C_KA // WA), col(C_VA // WA), col(C_QI // WA),
                  row(N_TAIL),
                  pl.BlockSpec((5, tm, LANES), lambda i: (0, i % n_tab, 0))],
        out_specs=[row(WA), row(WA), row(D_IDX), row(WA), row(WA), row(WA), row(WI),
                   row(LANES), row(LANES)],
        out_shape=[jax.ShapeDtypeStruct((m, WA), F32), jax.ShapeDtypeStruct((m, WA), F32),
                   jax.ShapeDtypeStruct((m, D_IDX), F32),
                   jax.ShapeDtypeStruct((m, WA), BF16), jax.ShapeDtypeStruct((m, WA), BF16),
                   jax.ShapeDtypeStruct((m, WA), BF16), jax.ShapeDtypeStruct((m, WI), BF16),
                   jax.ShapeDtypeStruct((m, LANES), BF16), jax.ShapeDtypeStruct((m, LANES), BF16)],
        compiler_params=_cparams(1),
        name="rope",
    )(z, z, z, z, tail, tab)


def _rope_tables(pos):
    pos = pos.astype(F32)[:, None]
    half_a, half_i = DH_A // 2, D_IDX // 2
    inv_a = ROPE_THETA ** (-jnp.arange(half_a, dtype=F32) / half_a)
    inv_i = ROPE_THETA ** (-jnp.arange(half_i, dtype=F32) / half_i)
    ca, sa = jnp.cos(pos * inv_a), jnp.sin(pos * inv_a)
    ci, si = jnp.cos(pos * inv_i), jnp.sin(pos * inv_i)
    zi = jnp.zeros_like(si)
    return jnp.stack([
        jnp.concatenate([ca, ca], axis=1),
        jnp.concatenate([-sa, sa], axis=1),
        jnp.concatenate([ci, ci, ci, ci], axis=1),
        jnp.concatenate([-si, zi, -si, zi], axis=1),
        jnp.concatenate([zi, si, zi, si], axis=1)])


IDX_W_SCALE = (H_IDX ** -0.5) * (D_IDX ** -0.5)


def _idx_scores(q_ref, w, kie, kio):
    nt = (((1,), (1,)), ((), ()))
    acc = jnp.zeros((q_ref.shape[0], kie.shape[0]), F32)
    for j in range(H_IDX // 2):
        qp = q_ref[:, j * LANES:(j + 1) * LANES]
        de = lax.dot_general(qp, kie, nt, preferred_element_type=F32)
        acc = acc + w[:, 2 * j:2 * j + 1] * jnp.maximum(de, 0.0)
        do = lax.dot_general(qp, kio, nt, preferred_element_type=F32)
        acc = acc + w[:, 2 * j + 1:2 * j + 2] * jnp.maximum(do, 0.0)
    return acc


def _idx_prompt_kernel(q_ref, tail_ref, kie_ref, kio_ref, o_ref, *, tq, tc, t_len):
    i = pl.program_id(1)
    w = tail_ref[:, L_WI:L_WI + H_IDX] * IDX_W_SCALE
    n_c = (i * tq + tq - 1) // tc + 1
    row = i * tq + lax.broadcasted_iota(I32, (tq, tc), 0)

    def body(c, carry):
        c0 = pl.multiple_of(c * tc, tc)
        acc = _idx_scores(q_ref, w, kie_ref[pl.ds(c0, tc), :], kio_ref[pl.ds(c0, tc), :])
        col = c0 + lax.broadcasted_iota(I32, (tq, tc), 1)
        o_ref[:, pl.ds(c0, tc)] = jnp.where(col <= row, acc, -jnp.inf)
        return carry

    lax.fori_loop(0, n_c, body, 0)

    def fill(c, carry):
        c0 = pl.multiple_of(c * tc, tc)
        o_ref[:, pl.ds(c0, tc)] = jnp.full((tq, tc), -jnp.inf, F32)
        return carry

    lax.fori_loop(n_c, t_len // tc, fill, 0)


def _idx_prompt(qi_bf, tail, kie, kio, n_batch, t_len, tq=128, tc=256):
    nq = t_len // tq
    kern = functools.partial(_idx_prompt_kernel, tq=tq, tc=tc, t_len=t_len)
    return pl.pallas_call(
        kern,
        grid=(n_batch, nq),
        in_specs=[pl.BlockSpec((tq, WI), lambda b, i: (b * nq + i, 0)),
                  pl.BlockSpec((tq, N_TAIL), lambda b, i: (b * nq + i, 0)),
                  pl.BlockSpec((t_len, LANES), lambda b, i: (b, 0)),
                  pl.BlockSpec((t_len, LANES), lambda b, i: (b, 0))],
        out_specs=pl.BlockSpec((tq, t_len), lambda b, i: (b * nq + i, 0)),
        out_shape=jax.ShapeDtypeStruct((n_batch * t_len, t_len), F32),
        compiler_params=_cparams(2),
        name="idx_prompt",
    )(qi_bf, tail, kie, kio)


SELECT_MAX_CHECKS = 24
SELECT_STEPS_PER_CHECK = 2


def _select_kernel(s_ref, o_ref, *, tr, sub, n_cols, k_top, rows_per_batch):
    tc = sub * LANES
    n_all = n_cols // tc
    if rows_per_batch:
        r0 = (pl.program_id(0) % (rows_per_batch // tr)) * tr
        n_c = (r0 + tr - 1) // tc + 1
    else:
        n_c = n_all
    kf = float(k_top)

    def stats(c, carry):
        mn, mx, cnt = carry
        c0 = pl.multiple_of(c * tc, tc)
        for u in range(sub):
            x = s_ref[:, pl.ds(c0 + u * LANES, LANES)]
            valid = x > -jnp.inf
            mn = jnp.minimum(mn, jnp.where(valid, x, jnp.inf))
            mx = jnp.maximum(mx, x)
            cnt = cnt + jnp.where(valid, 1.0, 0.0)
        return mn, mx, cnt

    mn, mx, cnt = lax.fori_loop(
        0, n_c, stats, (jnp.full((tr, LANES), jnp.inf, F32), jnp.full((tr, LANES), -jnp.inf, F32),
                        jnp.zeros((tr, LANES), F32)))
    lo0 = jnp.min(mn, axis=1, keepdims=True)
    hi0 = jnp.max(mx, axis=1, keepdims=True)
    c0_lo = jnp.sum(cnt, axis=1, keepdims=True)

    def count_ge(thr):
        thr_b = jnp.broadcast_to(thr, (tr, LANES))

        def body(c, acc):
            c0 = pl.multiple_of(c * tc, tc)
            for u in range(sub):
                x = s_ref[:, pl.ds(c0 + u * LANES, LANES)]
                acc = acc + jnp.where(x >= thr_b, 1.0, 0.0)
            return acc

        acc = lax.fori_loop(0, n_c, body, jnp.zeros((tr, LANES), F32))
        return jnp.sum(acc, axis=1, keepdims=True)

    def unresolved(state):
        it, _, _, c_lo = state
        return jnp.logical_and(it < SELECT_MAX_CHECKS, jnp.max(c_lo) > kf)

    def refine(state):
        it, lo, hi, c_lo = state
        for _ in range(SELECT_STEPS_PER_CHECK):
            mid = 0.5 * lo + 0.5 * hi
            c = count_ge(mid)
            ok = c >= kf
            lo = jnp.where(ok, mid, lo)
            c_lo = jnp.where(ok, c, c_lo)
            hi = jnp.where(ok, hi, mid)
        return it + 1, lo, hi, c_lo

    _, thr, _, _ = lax.while_loop(unresolved, refine, (jnp.int32(0), lo0, hi0, c0_lo))

    def emit(c, carry):
        c0 = pl.multiple_of(c * tc, tc)
        x = s_ref[:, pl.ds(c0, tc)]
        o_ref[:, pl.ds(c0, tc)] = jnp.where(x >= thr, 0.0, NEG_BIG).astype(o_ref.dtype)
        return carry

    lax.fori_loop(0, n_c, emit, 0)

    def fill(c, carry):
        c0 = pl.multiple_of(c * tc, tc)
        o_ref[:, pl.ds(c0, tc)] = jnp.full((tr, tc), NEG_BIG, o_ref.dtype)
        return carry

    lax.fori_loop(n_c, n_all, fill, 0)


def _select(scores, k_top, sub, rows_per_batch, out_dtype, tr=128):
    m, n_cols = scores.shape
    kern = functools.partial(_select_kernel, tr=tr, sub=sub, n_cols=n_cols, k_top=k_top,
                             rows_per_batch=rows_per_batch)
    return pl.pallas_call(
        kern,
        grid=(m // tr,),
        in_specs=[pl.BlockSpec((tr, n_cols), lambda i: (i, 0))],
        out_specs=pl.BlockSpec((tr, n_cols), lambda i: (i, 0)),
        out_shape=jax.ShapeDtypeStruct((m, n_cols), out_dtype),
        compiler_params=_cparams(1),
        name="select",
    )(scores)


def _attn_prompt_kernel(q_ref, k_ref, v_ref, b_ref, o_ref, m_ref, l_ref, acc_ref, *, tq, ts):
    i, j = pl.program_id(1), pl.program_id(2)
    n_rep = ts // LANES

    @pl.when(j == 0)
    def _():
        m_ref[...] = jnp.full(m_ref.shape, NEG_BIG, F32)
        l_ref[...] = jnp.zeros(l_ref.shape, F32)
        acc_ref[...] = jnp.zeros(acc_ref.shape, F32)

    @pl.when(j * ts <= i * tq + tq - 1)
    def _():
        bias = b_ref[...].astype(F32)
        nt = (((1,), (1,)), ((), ()))
        for h in range(H_A):
            sl = slice(h * DH_A, (h + 1) * DH_A)
            s = lax.dot_general(q_ref[:, sl], k_ref[:, sl], nt, preferred_element_type=F32) + bias
            m_prev = m_ref[h]
            m_next = jnp.maximum(m_prev, jnp.max(s, axis=1, keepdims=True))
            alpha = jnp.exp(m_prev - m_next)
            p = jnp.exp(s - jnp.tile(m_next, (1, n_rep)))
            l_ref[h] = alpha * l_ref[h] + jnp.sum(p, axis=1, keepdims=True)
            m_ref[h] = m_next
            pv = jnp.dot(p.astype(BF16), v_ref[:, sl], preferred_element_type=F32)
            acc_ref[:, sl] = acc_ref[:, sl] * alpha + pv

    @pl.when(j == pl.num_programs(2) - 1)
    def _():
        for h in range(H_A):
            sl = slice(h * DH_A, (h + 1) * DH_A)
            o_ref[:, sl] = (acc_ref[:, sl] / l_ref[h]).astype(o_ref.dtype)


def _attn_prompt(q_bf, k_bf, v_bf, bias, n_batch, t_len, tq=256, ts=512):
    nq, ns = t_len // tq, t_len // ts
    last = lambda i: (i * tq + tq - 1) // ts
    kern = functools.partial(_attn_prompt_kernel, tq=tq, ts=ts)
    return pl.pallas_call(
        kern,
        grid=(n_batch, nq, ns),
        in_specs=[pl.BlockSpec((tq, WA), lambda b, i, j: (b * nq + i, 0)),
                  pl.BlockSpec((ts, WA), lambda b, i, j: (b * ns + jnp.minimum(j, last(i)), 0)),
                  pl.BlockSpec((ts, WA), lambda b, i, j: (b * ns + jnp.minimum(j, last(i)), 0)),
                  pl.BlockSpec((tq, ts), lambda b, i, j: (b * nq + i, jnp.minimum(j, last(i))))],
        out_specs=pl.BlockSpec((tq, WA), lambda b, i, j: (b * nq + i, 0)),
        out_shape=jax.ShapeDtypeStruct((n_batch * t_len, WA), F32),
        scratch_shapes=[pltpu.VMEM((H_A, tq, LANES), F32), pltpu.VMEM((H_A, tq, LANES), F32),
                        pltpu.VMEM((tq, WA), F32)],
        compiler_params=_cparams(3),
        name="attn_prompt",
    )(q_bf, k_bf, v_bf, bias)


def _log_sigmoid(x):
    return jnp.minimum(x, 0.0) - jnp.log1p(jnp.exp(-jnp.abs(x)))


def _bf16_split3(x):
    x1 = x.astype(BF16)
    r1 = x - x1.astype(F32)
    x2 = r1.astype(BF16)
    x3 = (r1 - x2.astype(F32)).astype(BF16)
    return x1, x2, x3


def _gla_kernel(q_ref, k_ref, v_ref, tail_ref, wg_ref, bg_ref, s0_ref, o_ref, sfin_ref,
                st_ref, cum_all, kk_all, xs_all, *, tb, chunk, sub):
    c_id = pl.program_id(1)
    n_sub = chunk // sub
    nt = (((1,), (1,)), ((), ()))
    tn = (((0,), (0,)), ((), ()))

    @pl.when(c_id == 0)
    def _():
        for h in range(H_B):
            st_ref[h] = s0_ref[h].T

    rows = lax.broadcasted_iota(I32, (chunk, LANES), 0)
    cols = lax.broadcasted_iota(I32, (chunk, LANES), 1)
    tri = (lax.broadcasted_iota(I32, (chunk, chunk), 1)
           <= lax.broadcasted_iota(I32, (chunk, chunk), 0)).astype(BF16)
    ones = jnp.ones((LANES, LANES), BF16)

    def load(ref, r0, n_rows, c0, width):
        x = ref[pl.ds(r0, n_rows), c0:c0 + width]
        if n_rows < chunk:
            x = jnp.concatenate([x, jnp.zeros((chunk - n_rows, width), x.dtype)], axis=0)
        return x

    def do_chunk(r0, n_rows):
        ab = load(tail_ref, r0, n_rows, 0, N_TAIL)[:, L_AB:L_AB + GATE_RANK].astype(BF16)
        for h in range(H_B):
            do_head(h, r0, n_rows, ab)

    def do_head(h, r0, n_rows, ab):
        cum_ref, kk_ref, xs_ref = cum_all.at[h], kk_all.at[h], xs_all.at[h]
        q = load(q_ref, r0, n_rows, h * DK_B, DK_B) * (DK_B ** -0.5)
        k = load(k_ref, r0, n_rows, h * DK_B, DK_B)
        v = load(v_ref, r0, n_rows, h * DV_B, DV_B)
        wg = wg_ref[:, h * DK_B:(h + 1) * DK_B].astype(BF16)
        bg = bg_ref[:, h * DK_B:(h + 1) * DK_B]
        la = _log_sigmoid(jnp.dot(ab, wg, preferred_element_type=F32) + bg) / GATE_TAU
        if n_rows < chunk:
            la = jnp.where(rows < n_rows, la, 0.0)
        l1, l2, l3 = _bf16_split3(la)
        cum = (jnp.dot(tri, l1, preferred_element_type=F32)
               + jnp.dot(tri, l2, preferred_element_type=F32)
               + jnp.dot(tri, l3, preferred_element_type=F32))
        cum_ref[...] = cum
        kk_ref[...] = k
        st = st_ref[h]
        vb = v.astype(BF16)

        o = lax.dot_general((q * jnp.exp(cum)).astype(BF16), st.astype(BF16), nt,
                            preferred_element_type=F32)

        att = jnp.zeros((chunk, LANES), F32)
        for i in range(1, n_sub):
            r_i = cum_ref[i * sub - 1:i * sub, :]
            qt = (q * jnp.exp(jnp.minimum(cum - r_i, 0.0))).astype(BF16)
            kt = (k * jnp.exp(jnp.minimum(r_i - cum, 0.0))).astype(BF16)
            a_i = lax.dot_general(qt, kt, nt, preferred_element_type=F32)
            if chunk < LANES:
                a_i = jnp.concatenate([a_i, jnp.zeros((chunk, LANES - chunk), F32)], axis=1)
            att = jnp.where((rows // sub == i) & (cols < i * sub), a_i, att)

        for sl in range(sub):
            cum_s = jnp.concatenate(
                [jnp.broadcast_to(cum_ref[i * sub + sl:i * sub + sl + 1, :], (sub, LANES))
                 for i in range(n_sub)], axis=0)
            k_s = jnp.concatenate(
                [jnp.broadcast_to(kk_ref[i * sub + sl:i * sub + sl + 1, :], (sub, LANES))
                 for i in range(n_sub)], axis=0)
            dec = jnp.where(rows % sub >= sl, jnp.exp(jnp.minimum(cum - cum_s, 0.0)), 0.0)
            xs_ref[sl * chunk:(sl + 1) * chunk, :] = (q * k_s * dec).astype(BF16)
        red = jnp.dot(xs_ref[...], ones, preferred_element_type=F32)
        for sl in range(sub):
            hit = (cols % sub == sl) & (cols // sub == rows // sub) & (cols < chunk)
            att = jnp.where(hit, red[sl * chunk:(sl + 1) * chunk, :], att)

        o = o + jnp.dot(att[:, :chunk].astype(BF16), vb, preferred_element_type=F32)
        o_ref[pl.ds(r0, n_rows), h * DV_B:(h + 1) * DV_B] = o[:n_rows]

        last = cum_ref[chunk - 1:chunk, :]
        kd = (k * jnp.exp(last - cum)).astype(BF16)
        st_ref[h] = st * jnp.exp(last) + lax.dot_general(vb, kd, tn, preferred_element_type=F32)

    if tb < chunk:
        do_chunk(0, tb)
    else:
        def body(c, carry):
            do_chunk(pl.multiple_of(c * chunk, chunk), chunk)
            return carry
        lax.fori_loop(0, tb // chunk, body, 0)

    @pl.when(c_id == pl.num_programs(1) - 1)
    def _():
        for h in range(H_B):
            sfin_ref[h] = st_ref[h].T


def _gla(z, tail, w_gate_up, b_gate, s0, n_batch, t_len, tb, chunk, sub):
    nb = t_len // tb
    kern = functools.partial(_gla_kernel, tb=tb, chunk=chunk, sub=sub)
    return pl.pallas_call(
        kern,
        grid=(n_batch, nb),
        in_specs=[pl.BlockSpec((tb, WKB), lambda b, c: (b * nb + c, C_QB // WKB)),
                  pl.BlockSpec((tb, WKB), lambda b, c: (b * nb + c, C_KB // WKB)),
                  pl.BlockSpec((tb, WVB), lambda b, c: (b * nb + c, C_VB // WVB)),
                  pl.BlockSpec((tb, N_TAIL), lambda b, c: (b * nb + c, 0)),
                  pl.BlockSpec((GATE_RANK, WKB), lambda b, c: (0, 0)),
                  pl.BlockSpec((1, WKB), lambda b, c: (0, 0)),
                  pl.BlockSpec((None, H_B, DK_B, DV_B), lambda b, c: (b, 0, 0, 0))],
        out_specs=[pl.BlockSpec((tb, WVB), lambda b, c: (b * nb + c, 0)),
                   pl.BlockSpec((None, H_B, DK_B, DV_B), lambda b, c: (b, 0, 0, 0))],
        out_shape=[jax.ShapeDtypeStruct((n_batch * t_len, WVB), F32),
                   jax.ShapeDtypeStruct((n_batch, H_B, DK_B, DV_B), F32)],
        scratch_shapes=[pltpu.VMEM((H_B, DV_B, DK_B), F32), pltpu.VMEM((H_B, chunk, LANES), F32),
                        pltpu.VMEM((H_B, chunk, LANES), F32),
                        pltpu.VMEM((H_B, sub * chunk, LANES), BF16)],
        compiler_params=_cparams(2),
        name="gla",
    )(z, z, z, tail, w_gate_up, b_gate, s0)


def _silu(x):
    return x / (1.0 + jnp.exp(-x))


def _merge_kernel(att_ref, ga_ref, ob_ref, gb_ref, gn_ref, wo_ref, x_ref, nf_ref, o_ref):
    a = (att_ref[...] * _silu(ga_ref[...])).astype(BF16)
    y = jnp.dot(a, wo_ref[:WA, :], preferred_element_type=F32)
    gn = gn_ref[...]
    for h in range(H_B):
        sl = slice(h * DV_B, (h + 1) * DV_B)
        ob = ob_ref[:, sl]
        bn = ob * lax.rsqrt(jnp.mean(ob * ob, axis=-1, keepdims=True) + EPS) * gn
        bp = (bn * _silu(gb_ref[:, sl])).astype(BF16)
        y = y + jnp.dot(bp, wo_ref[WA + h * DV_B:WA + (h + 1) * DV_B, :], preferred_element_type=F32)
    xo = x_ref[...] + y
    o_ref[...] = xo * lax.rsqrt(jnp.mean(xo * xo, axis=-1, keepdims=True) + EPS) * nf_ref[...]


def _merge(att, z, ob, gla_norm, w_out_bf, x2d, norm_f, tm):
    m = x2d.shape[0]
    return pl.pallas_call(
        _merge_kernel,
        grid=(m // tm,),
        in_specs=[pl.BlockSpec((tm, WA), lambda i: (i, 0)),
                  pl.BlockSpec((tm, WA), lambda i: (i, C_GA // WA)),
                  pl.BlockSpec((tm, WVB), lambda i: (i, 0)),
                  pl.BlockSpec((tm, WVB), lambda i: (i, C_GB // WVB)),
                  pl.BlockSpec((1, DV_B), lambda i: (0, 0)),
                  pl.BlockSpec((WA + WVB, D_MODEL), lambda i: (0, 0)),
                  pl.BlockSpec((tm, D_MODEL), lambda i: (i, 0)),
                  pl.BlockSpec((1, D_MODEL), lambda i: (0, 0))],
        out_specs=pl.BlockSpec((tm, D_MODEL), lambda i: (i, 0)),
        out_shape=jax.ShapeDtypeStruct((m, D_MODEL), F32),
        compiler_params=_cparams(1),
        name="merge",
    )(att, z, ob, z, gla_norm, w_out_bf, x2d, norm_f)


PAGES_PER_STEP = 8


def _idx_pages_kernel(pt_ref, q_ref, w_ref, *refs, n_pages, t_dec, causal):
    page_refs, o_ref = refs[:n_pages], refs[n_pages]
    nt = (((1,), (1,)), ((), ()))
    q = q_ref[...]
    w = w_ref[...]
    for p in range(n_pages):
        kp = page_refs[p][...].astype(BF16)
        d = lax.dot_general(q, kp, nt, preferred_element_type=F32)
        r = w * jnp.maximum(d, 0.0)
        acc = r[0:t_dec]
        for h in range(1, H_IDX):
            acc = acc + r[h * t_dec:(h + 1) * t_dec]
        if causal:
            t_i = lax.broadcasted_iota(I32, acc.shape, 0)
            s_i = lax.broadcasted_iota(I32, acc.shape, 1)
            acc = jnp.where(s_i <= t_i, acc, -jnp.inf)
        o_ref[:, p * PAGE_SIZE:(p + 1) * PAGE_SIZE] = acc


def _idx_sample_past(page_table, q_stk, w_stk, pool_ik, n_seq, t_dec, n_pages):
    pps = PAGES_PER_STEP
    n_g = n_pages // pps
    rows = H_IDX * t_dec
    kern = functools.partial(_idx_pages_kernel, n_pages=pps, t_dec=t_dec, causal=False)
    page_spec = lambda r: pl.BlockSpec(
        (None, PAGE_SIZE, D_IDX), lambda b, g, pt, r=r: (pt[b * n_pages + g * pps + r], 0, 0))
    return pl.pallas_call(
        kern,
        grid_spec=pltpu.PrefetchScalarGridSpec(
            num_scalar_prefetch=1,
            grid=(n_seq, n_g),
            in_specs=[pl.BlockSpec((rows, D_IDX), lambda b, g, pt: (b, 0)),
                      pl.BlockSpec((rows, 1), lambda b, g, pt: (b, 0))]
                     + [page_spec(r) for r in range(pps)],
            out_specs=pl.BlockSpec((t_dec, pps * PAGE_SIZE), lambda b, g, pt: (b, g))),
        out_shape=jax.ShapeDtypeStruct((n_seq * t_dec, n_pages * PAGE_SIZE), F32),
        compiler_params=_cparams(2),
        name="idx_sample_past",
    )(page_table.reshape(-1), q_stk, w_stk, *([pool_ik] * pps))


def _idx_sample_new(q_stk, w_stk, ki_new_pad, n_seq, t_dec):
    rows = H_IDX * t_dec
    kern = functools.partial(_idx_pages_kernel, None, n_pages=1, t_dec=t_dec, causal=True)
    return pl.pallas_call(
        kern,
        grid=(n_seq,),
        in_specs=[pl.BlockSpec((rows, D_IDX), lambda b: (b, 0)),
                  pl.BlockSpec((rows, 1), lambda b: (b, 0)),
                  pl.BlockSpec((None, PAGE_SIZE, D_IDX), lambda b: (b, 0, 0))],
        out_specs=pl.BlockSpec((t_dec, PAGE_SIZE), lambda b: (b, 0)),
        out_shape=jax.ShapeDtypeStruct((n_seq * t_dec, PAGE_SIZE), F32),
        compiler_params=_cparams(1),
        name="idx_sample_new",
    )(q_stk, w_stk, ki_new_pad)


ATTN_PAGES_PER_STEP = 8


def _attn_sample_kernel(pt_ref, q_ref, sel_ref, selnew_ref, e_ref, knew_ref, vnew_ref, *refs,
                        n_pages, t_dec):
    k_refs, v_refs = refs[:n_pages], refs[n_pages:2 * n_pages]
    o_ref, m_ref, l_ref, acc_ref = refs[2 * n_pages:]
    g = pl.program_id(1)
    rows = H_A * t_dec
    nt = (((1,), (1,)), ((), ()))
    q = q_ref[...]
    e = e_ref[...]
    n_col = PAGE_SIZE * H_A
    head_ok = (lax.broadcasted_iota(I32, (rows, n_col), 0) // t_dec
               == lax.broadcasted_iota(I32, (rows, n_col), 1) % H_A)

    def update(sels, k_pages, v_pages):
        scores = []
        m_next = m_ref[...]
        for sel, k_page in zip(sels, k_pages):
            picked = jnp.where(jnp.tile(sel, (H_A, 1)) == 0.0, 1.0, 0.0).astype(BF16)
            picked = jnp.dot(picked, e, preferred_element_type=F32)
            s = lax.dot_general(q, k_page[...].astype(BF16), nt, preferred_element_type=F32)
            s = jnp.where(head_ok, jnp.where(picked > 0.5, s, NEG_BIG), NEG_BIG)
            m_next = jnp.maximum(m_next, jnp.max(s, axis=1, keepdims=True))
            scores.append(s)
        alpha = jnp.exp(m_ref[...] - m_next)
        m_rep = jnp.tile(m_next, (1, n_col // LANES))
        l_new = alpha * l_ref[...]
        acc = acc_ref[...] * alpha
        for s, v_page in zip(scores, v_pages):
            p = jnp.exp(s - m_rep)
            l_new = l_new + jnp.sum(p, axis=1, keepdims=True)
            acc = acc + jnp.dot(p.astype(BF16), v_page[...].astype(BF16), preferred_element_type=F32)
        m_ref[...] = m_next
        l_ref[...] = l_new
        acc_ref[...] = acc

    @pl.when(g == 0)
    def _():
        m_ref[...] = jnp.full(m_ref.shape, NEG_BIG, F32)
        l_ref[...] = jnp.zeros(l_ref.shape, F32)
        acc_ref[...] = jnp.zeros(acc_ref.shape, F32)
        update([selnew_ref[...]], [knew_ref], [vnew_ref])

    update([sel_ref[:, p_i * PAGE_SIZE:(p_i + 1) * PAGE_SIZE] for p_i in range(n_pages)],
           k_refs, v_refs)

    @pl.when(g == pl.num_programs(1) - 1)
    def _():
        o_ref[...] = acc_ref[...] / l_ref[...]


def _attn_sample(page_table, q_stk, sel_past, sel_new, expand, knew_pad, vnew_pad, pool_k, pool_v,
                 n_seq, t_dec, n_pages):
    pps = ATTN_PAGES_PER_STEP
    n_g = n_pages // pps
    rows = H_A * t_dec
    n_col = PAGE_SIZE * H_A
    kern = functools.partial(_attn_sample_kernel, n_pages=pps, t_dec=t_dec)
    page_spec = lambda r: pl.BlockSpec(
        (None, n_col, DH_A), lambda b, g, pt, r=r: (pt[b * n_pages + g * pps + r], 0, 0))
    return pl.pallas_call(
        kern,
        grid_spec=pltpu.PrefetchScalarGridSpec(
            num_scalar_prefetch=1,
            grid=(n_seq, n_g),
            in_specs=[pl.BlockSpec((rows, DH_A), lambda b, g, pt: (b, 0)),
                      pl.BlockSpec((t_dec, pps * PAGE_SIZE), lambda b, g, pt: (b, g)),
                      pl.BlockSpec((t_dec, PAGE_SIZE), lambda b, g, pt: (b, 0)),
                      pl.BlockSpec((PAGE_SIZE, n_col), lambda b, g, pt: (0, 0)),
                      pl.BlockSpec((None, n_col, DH_A), lambda b, g, pt: (b, 0, 0)),
                      pl.BlockSpec((None, n_col, DH_A), lambda b, g, pt: (b, 0, 0))]
                     + [page_spec(r) for r in range(pps)] * 2,
            out_specs=pl.BlockSpec((rows, DH_A), lambda b, g, pt: (b, 0)),
            scratch_shapes=[pltpu.VMEM((rows, LANES), F32), pltpu.VMEM((rows, LANES), F32),
                            pltpu.VMEM((rows, DH_A), F32)]),
        out_shape=jax.ShapeDtypeStruct((n_seq * rows, DH_A), F32),
        compiler_params=_cparams(2),
        name="attn_sample",
    )(page_table.reshape(-1), q_stk, sel_past, sel_new, expand, knew_pad, vnew_pad,
      *([pool_k] * pps), *([pool_v] * pps))


def _group_common(x, norm_in, w_main, w_tail, pos, tm, tn, tr):
    n_seq, t_len, _ = x.shape
    x2d = x.reshape(n_seq * t_len, D_MODEL)
    z = _proj(x2d, norm_in, w_main, tm, tn)
    tail = _proj(x2d, norm_in, w_tail, tm, N_TAIL)
    tab = _rope_tables(pos)
    return x2d, z, tail, _rope(z, tail, tab, tr)


def kernel(x_prompt, x_sample, cache_k, cache_v, cache_idx_k, state_gla, page_table,
           norm_in, w_in, w_gate_up, b_gate, gla_norm, w_out, norm_f):
    n_p, t_p, _ = x_prompt.shape
    n_s, t_s, _ = x_sample.shape
    n_pages = page_table.shape[1]
    past = n_pages * PAGE_SIZE
    assert cache_k.shape[0] == 1, "single layer"

    w_main, w_tail = _permute_w_in(w_in[0])
    w_out_bf = w_out[0].astype(BF16)
    g_in, g_gla, g_f = norm_in[0][None], gla_norm[0][None], norm_f[None]
    wg, bg = w_gate_up[0], b_gate[0][None]

    xp2d, zp, tailp, (newk_p, newv_p, newik_p, q_p, k_p, v_p, qi_p, kie_p, kio_p) = _group_common(
        x_prompt, g_in, w_main, w_tail, jnp.arange(t_p), 1024, 512, 256)
    sc_p = _idx_prompt(qi_p, tailp, kie_p, kio_p, n_p, t_p)
    bias_p = _select(sc_p, min(TOPK_MAX, t_p // 4), 4, t_p, BF16)
    att_p = _attn_prompt(q_p, k_p, v_p, bias_p, n_p, t_p)
    s0_p = jnp.zeros((n_p, H_B, DK_B, DV_B), F32)
    ob_p, sfin_p = _gla(zp, tailp, wg, bg, s0_p, n_p, t_p, 256, 64, 16)
    y_p = _merge(att_p, zp, ob_p, g_gla, w_out_bf, xp2d, g_f, 256)

    m_s = n_s * t_s
    pos_s = jnp.tile(past + jnp.arange(t_s), n_s)
    xs2d, zs, tails, (newk_s, newv_s, newik_s, q_s, k_s, v_s, qi_s, kie_s, kio_s) = _group_common(
        x_sample, g_in, w_main, w_tail, pos_s, m_s, 512, m_s)
    qi_stk = qi_s.reshape(n_s, t_s, H_IDX, D_IDX).transpose(0, 2, 1, 3).reshape(n_s * H_IDX * t_s, D_IDX)
    w_stk = (tails[:, L_WI:L_WI + H_IDX] * IDX_W_SCALE).reshape(n_s, t_s, H_IDX).transpose(0, 2, 1)
    w_stk = w_stk.reshape(n_s * H_IDX * t_s, 1)
    q_stk = q_s.reshape(n_s, t_s, H_A, DH_A).transpose(0, 2, 1, 3).reshape(n_s * H_A * t_s, DH_A)
    pad_rows = lambda a, n: jnp.pad(a, ((0, 0), (0, n - a.shape[1]), (0, 0)))
    ki_new_pad = pad_rows(newik_s.reshape(n_s, t_s, D_IDX), PAGE_SIZE)
    knew_pad = pad_rows(newk_s.reshape(n_s, t_s * H_A, DH_A), PAGE_SIZE * H_A)
    vnew_pad = pad_rows(newv_s.reshape(n_s, t_s * H_A, DH_A), PAGE_SIZE * H_A)
    pool_ik = cache_idx_k[0]
    pool_k = cache_k[0].reshape(-1, PAGE_SIZE * H_A, DH_A)
    pool_v = cache_v[0].reshape(-1, PAGE_SIZE * H_A, DH_A)

    sc_past = _idx_sample_past(page_table, qi_stk, w_stk, pool_ik, n_s, t_s, n_pages)
    sc_new = _idx_sample_new(qi_stk, w_stk, ki_new_pad, n_s, t_s)
    sc_s = jnp.concatenate([sc_past, sc_new], axis=1)
    k_top_s = min(TOPK_MAX, (past + t_s) // 4)
    n_tiles_s = sc_s.shape[1] // LANES
    sub_s = max(d for d in range(1, 6) if n_tiles_s % d == 0)
    bias_s = _select(sc_s, k_top_s, sub_s, 0, F32)
    expand = (jnp.arange(PAGE_SIZE * H_A)[None, :] // H_A == jnp.arange(PAGE_SIZE)[:, None]).astype(BF16)
    att_stk = _attn_sample(page_table, q_stk, bias_s[:, :past], bias_s[:, past:], expand,
                           knew_pad, vnew_pad, pool_k, pool_v, n_s, t_s, n_pages)
    att_s = att_stk.reshape(n_s, H_A, t_s, DH_A).transpose(0, 2, 1, 3).reshape(m_s, WA)
    ob_s, sfin_s = _gla(zs, tails, wg, bg, state_gla[0], n_s, t_s, t_s, 16, 16)
    y_s = _merge(att_s, zs, ob_s, g_gla, w_out_bf, xs2d, g_f, m_s)

    return (y_p.reshape(n_p, t_p, D_MODEL), y_s.reshape(n_s, t_s, D_MODEL),
            newk_p.reshape(1, n_p, t_p, H_A, DH_A), newv_p.reshape(1, n_p, t_p, H_A, DH_A),
            newik_p.reshape(1, n_p, t_p, D_IDX), sfin_p[None],
            newk_s.reshape(1, n_s, t_s, H_A, DH_A), newv_s.reshape(1, n_s, t_s, H_A, DH_A),
            newik_s.reshape(1, n_s, t_s, D_IDX), sfin_s[None])
```

```python
import functools
import math

import jax
import jax.numpy as jnp
import numpy as np
from jax import lax
from jax.experimental import pallas as pl
from jax.experimental.pallas import tpu as pltpu

F32 = jnp.float32
BF16 = jnp.bfloat16
I32 = jnp.int32

D_MODEL = 2048
PAGE_SIZE = 128
DH_A = 128
H_A = 8
H_IDX = 16
D_IDX = 64
TOPK_MAX = 256
DK_B = 128
DV_B = 256
H_B = 4
GATE_RANK = 16
GATE_TAU = 16.0
ROPE_THETA = 10000.0
EPS = 1e-6
NEG_BIG = -1e30
WA = H_A * DH_A
WI = H_IDX * D_IDX
WKB = H_B * DK_B
WVB = H_B * DV_B

C_QA, C_KA, C_VA, C_GA, C_QI = 0, WA, 2 * WA, 3 * WA, 4 * WA
C_QB = C_QI + WI
C_KB = C_QB + WKB
C_VB = C_KB + WKB
C_GB = C_VB + WVB
N_MAIN = C_GB + WVB
L_KI, L_WI, L_AB = 0, D_IDX, D_IDX + H_IDX
N_TAIL = 128

LANES = 128
INT_MIN = -2 ** 31
VMEM_LIMIT = 48 * 1024 * 1024


def _prep_w_in(w_in):
    o_ki = C_QI + WI
    o_qb = o_ki + D_IDX + H_IDX
    o_ab = o_qb + 2 * WKB + 2 * WVB
    w_bf = w_in.astype(BF16)
    pad = jnp.zeros((w_in.shape[0], N_TAIL - D_IDX - H_IDX - GATE_RANK), BF16)
    tail = jnp.concatenate([w_bf[:, o_ki:o_qb], w_bf[:, o_ab:], pad], axis=1)
    return w_bf, w_bf[:, o_qb:o_ab], tail


def _cparams(n_axes):
    return pltpu.CompilerParams(dimension_semantics=("arbitrary",) * n_axes,
                                vmem_limit_bytes=VMEM_LIMIT)


def _proj_kernel(x_ref, g_ref, wa_ref, wb_ref, o_ref, h_ref, *, n_a):
    j = pl.program_id(1)

    @pl.when(j == 0)
    def _():
        x = x_ref[...]
        ms = jnp.mean(x * x, axis=-1, keepdims=True)
        h_ref[...] = (x * lax.rsqrt(ms + EPS) * g_ref[...]).astype(BF16)

    @pl.when(j < n_a)
    def _():
        o_ref[...] = jnp.dot(h_ref[...], wa_ref[...], preferred_element_type=F32)

    @pl.when(j >= n_a)
    def _():
        o_ref[...] = jnp.dot(h_ref[...], wb_ref[...], preferred_element_type=F32)


def _proj(x2d, g, w_a, n_cols_a, w_b, tm, tn):
    m, k = x2d.shape
    n_a = n_cols_a // tn
    if w_b is None:
        w_b, n_b = w_a, 0
    else:
        n_b = w_b.shape[1] // tn
    return pl.pallas_call(
        functools.partial(_proj_kernel, n_a=n_a),
        grid=(m // tm, n_a + n_b),
        in_specs=[pl.BlockSpec((tm, k), lambda i, j: (i, 0)),
                  pl.BlockSpec((1, k), lambda i, j: (0, 0)),
                  pl.BlockSpec((k, tn), lambda i, j: (0, jnp.minimum(j, n_a - 1))),
                  pl.BlockSpec((k, tn), lambda i, j: (0, jnp.maximum(j - n_a, 0)))],
        out_specs=pl.BlockSpec((tm, tn), lambda i, j: (i, j)),
        out_shape=jax.ShapeDtypeStruct((m, (n_a + n_b) * tn), F32),
        scratch_shapes=[pltpu.VMEM((tm, k), BF16)],
        compiler_params=_cparams(2),
        name="proj",
    )(x2d, g, w_a, w_b)


def _rope128(x, cos, sin_signed):
    return x * cos + pltpu.roll(x, 64, axis=1) * sin_signed


def _rope64(x, cos, sin_lo, sin_hi):
    return x * cos + pltpu.roll(x, 96, axis=1) * sin_lo + pltpu.roll(x, 32, axis=1) * sin_hi


def _rope_kernel(qa_ref, ka_ref, va_ref, qi_ref, tail_ref, tab_ref,
                 newk_ref, newv_ref, newik_ref, qbf_ref, kbf_ref, vbf_ref, qibf_ref,
                 kie_ref, kio_ref):
    cos_a, sin_a = tab_ref[0], tab_ref[1]
    cos_i, sin_lo, sin_hi = tab_ref[2], tab_ref[3], tab_ref[4]
    scale = (DH_A ** -0.5) * math.log2(math.e)
    for h in range(H_A):
        sl = slice(h * DH_A, (h + 1) * DH_A)
        q = _rope128(qa_ref[:, sl], cos_a, sin_a)
        qbf_ref[:, sl] = (q * scale).astype(BF16)
        k = _rope128(ka_ref[:, sl], cos_a, sin_a)
        newk_ref[:, sl] = k
        kbf_ref[:, sl] = k.astype(BF16)
    v = va_ref[...]
    newv_ref[...] = v
    vbf_ref[...] = v.astype(BF16)
    for j in range(WI // LANES):
        sl = slice(j * LANES, (j + 1) * LANES)
        qibf_ref[:, sl] = _rope64(qi_ref[:, sl], cos_i, sin_lo, sin_hi).astype(BF16)
    ki = _rope64(tail_ref[...], cos_i, sin_lo, sin_hi)
    newik_ref[...] = ki[:, L_KI:L_KI + D_IDX]
    lane = lax.broadcasted_iota(I32, ki.shape, 1)
    ki_lo = jnp.where(lane < D_IDX, ki, 0.0)
    kie_ref[...] = ki_lo.astype(BF16)
    kio_ref[...] = pltpu.roll(ki_lo, D_IDX, axis=1).astype(BF16)


def _rope(z, tail, tab, tm):
    m = z.shape[0]
    n_tab = tab.shape[1] // tm
    col = lambda c: pl.BlockSpec((tm, WA), lambda i, c=c: (i, c))
    row = lambda w: pl.BlockSpec((tm, w), lambda i: (i, 0))
    return pl.pallas_call(
        _rope_kernel,
        grid=(m // tm,),
        in_specs=[col(C_QA // WA), col(C_KA // WA), col(C_VA // WA), col(C_QI // WA),
                  row(N_TAIL),
                  pl.BlockSpec((5, tm, LANES), lambda i: (0, i % n_tab, 0))],
        out_specs=[row(WA), row(WA), row(D_IDX), row(WA), row(WA), row(WA), row(WI),
                   row(LANES), row(LANES)],
        out_shape=[jax.ShapeDtypeStruct((m, WA), F32), jax.ShapeDtypeStruct((m, WA), F32),
                   jax.ShapeDtypeStruct((m, D_IDX), F32),
                   jax.ShapeDtypeStruct((m, WA), BF16), jax.ShapeDtypeStruct((m, WA), BF16),
                   jax.ShapeDtypeStruct((m, WA), BF16), jax.ShapeDtypeStruct((m, WI), BF16),
                   jax.ShapeDtypeStruct((m, LANES), BF16), jax.ShapeDtypeStruct((m, LANES), BF16)],
        compiler_params=_cparams(1),
        name="rope",
    )(z, z, z, z, tail, tab)


def _rope_tables(pos):
    pos = pos.astype(F32)[:, None]
    half_a, half_i = DH_A // 2, D_IDX // 2
    inv_a = ROPE_THETA ** (-jnp.arange(half_a, dtype=F32) / half_a)
    inv_i = ROPE_THETA ** (-jnp.arange(half_i, dtype=F32) / half_i)
    ca, sa = jnp.cos(pos * inv_a), jnp.sin(pos * inv_a)
    ci, si = jnp.cos(pos * inv_i), jnp.sin(pos * inv_i)
    zi = jnp.zeros_like(si)
    return jnp.stack([
        jnp.concatenate([ca, ca], axis=1),
        jnp.concatenate([-sa, sa], axis=1),
        jnp.concatenate([ci, ci, ci, ci], axis=1),
        jnp.concatenate([-si, zi, -si, zi], axis=1),
        jnp.concatenate([zi, si, zi, si], axis=1)])


IDX_W_SCALE = (H_IDX ** -0.5) * (D_IDX ** -0.5)


def _idx_scores(q_ref, w, kie, kio):
    nt = (((1,), (1,)), ((), ()))
    acc = jnp.zeros((q_ref.shape[0], kie.shape[0]), F32)
    for j in range(H_IDX // 2):
        qp = q_ref[:, j * LANES:(j + 1) * LANES]
        de = lax.dot_general(qp, kie, nt, preferred_element_type=F32)
        acc = acc + w[:, 2 * j:2 * j + 1] * jnp.maximum(de, 0.0)
        do = lax.dot_general(qp, kio, nt, preferred_element_type=F32)
        acc = acc + w[:, 2 * j + 1:2 * j + 2] * jnp.maximum(do, 0.0)
    return acc


def _idx_prompt_kernel(q_ref, tail_ref, kie_ref, kio_ref, o_ref, *, tq, tc, t_len):
    i = pl.program_id(1)
    w = tail_ref[:, L_WI:L_WI + H_IDX] * IDX_W_SCALE
    n_c = (i * tq + tq - 1) // tc + 1
    row = i * tq + lax.broadcasted_iota(I32, (tq, tc), 0)

    def body(c, carry):
        c0 = pl.multiple_of(c * tc, tc)
        acc = _idx_scores(q_ref, w, kie_ref[pl.ds(c0, tc), :], kio_ref[pl.ds(c0, tc), :])
        col = c0 + lax.broadcasted_iota(I32, (tq, tc), 1)
        o_ref[:, pl.ds(c0, tc)] = jnp.where(col <= row, acc, -jnp.inf)
        return carry

    lax.fori_loop(0, n_c, body, 0)

    def fill(c, carry):
        c0 = pl.multiple_of(c * tc, tc)
        o_ref[:, pl.ds(c0, tc)] = jnp.full((tq, tc), -jnp.inf, F32)
        return carry

    lax.fori_loop(n_c, t_len // tc, fill, 0)


def _idx_prompt(qi_bf, tail, kie, kio, n_batch, t_len, tq=128, tc=256):
    nq = t_len // tq
    kern = functools.partial(_idx_prompt_kernel, tq=tq, tc=tc, t_len=t_len)
    return pl.pallas_call(
        kern,
        grid=(n_batch, nq),
        in_specs=[pl.BlockSpec((tq, WI), lambda b, i: (b * nq + i, 0)),
                  pl.BlockSpec((tq, N_TAIL), lambda b, i: (b * nq + i, 0)),
                  pl.BlockSpec((t_len, LANES), lambda b, i: (b, 0)),
                  pl.BlockSpec((t_len, LANES), lambda b, i: (b, 0))],
        out_specs=pl.BlockSpec((tq, t_len), lambda b, i: (b * nq + i, 0)),
        out_shape=jax.ShapeDtypeStruct((n_batch * t_len, t_len), F32),
        compiler_params=_cparams(2),
        name="idx_prompt",
    )(qi_bf, tail, kie, kio)


SELECT_STEPS_PER_CHECK = 3
SELECT_MAX_CHECKS = 16


def _select_kernel(s_ref, o_ref, *, tr, sub, n_cols, k_top, rows_per_batch):
    tc = sub * LANES
    n_all = n_cols // tc
    if rows_per_batch:
        r0 = (pl.program_id(0) % (rows_per_batch // tr)) * tr
        n_c = (r0 + tr - 1) // tc + 1
    else:
        n_c = n_all
    kf = float(k_top)

    def stats(c, carry):
        mn, mx, cnt = carry
        c0 = pl.multiple_of(c * tc, tc)
        for u in range(sub):
            x = s_ref[:, pl.ds(c0 + u * LANES, LANES)]
            valid = x > -jnp.inf
            mn = jnp.minimum(mn, jnp.where(valid, x, jnp.inf))
            mx = jnp.maximum(mx, x)
            cnt = cnt + jnp.where(valid, 1.0, 0.0)
        return mn, mx, cnt

    mn, mx, cnt = lax.fori_loop(
        0, n_c, stats, (jnp.full((tr, LANES), jnp.inf, F32), jnp.full((tr, LANES), -jnp.inf, F32),
                        jnp.zeros((tr, LANES), F32)))
    lo0 = jnp.min(mn, axis=1, keepdims=True)
    hi0 = jnp.max(mx, axis=1, keepdims=True)
    c0_lo = jnp.sum(cnt, axis=1, keepdims=True)

    def count_ge(thr):
        thr_b = jnp.broadcast_to(thr, (tr, LANES))

        def body(c, acc):
            c0 = pl.multiple_of(c * tc, tc)
            for u in range(sub):
                x = s_ref[:, pl.ds(c0 + u * LANES, LANES)]
                acc = acc + jnp.where(x >= thr_b, 1.0, 0.0)
            return acc

        acc = lax.fori_loop(0, n_c, body, jnp.zeros((tr, LANES), F32))
        return jnp.sum(acc, axis=1, keepdims=True)

    def unresolved(state):
        it, _, _, c_lo = state
        return jnp.logical_and(it < SELECT_MAX_CHECKS, jnp.max(c_lo) > kf)

    def refine(state):
        it, lo, hi, c_lo = state
        for _ in range(SELECT_STEPS_PER_CHECK):
            mid = 0.5 * lo + 0.5 * hi
            c = count_ge(mid)
            ok = c >= kf
            lo = jnp.where(ok, mid, lo)
            c_lo = jnp.where(ok, c, c_lo)
            hi = jnp.where(ok, hi, mid)
        return it + 1, lo, hi, c_lo

    _, thr, _, _ = lax.while_loop(unresolved, refine, (jnp.int32(0), lo0, hi0, c0_lo))

    def emit(c, carry):
        c0 = pl.multiple_of(c * tc, tc)
        x = s_ref[:, pl.ds(c0, tc)]
        o_ref[:, pl.ds(c0, tc)] = jnp.where(x >= thr, 0.0, NEG_BIG).astype(o_ref.dtype)
        return carry

    lax.fori_loop(0, n_c, emit, 0)

    def fill(c, carry):
        c0 = pl.multiple_of(c * tc, tc)
        o_ref[:, pl.ds(c0, tc)] = jnp.full((tr, tc), NEG_BIG, o_ref.dtype)
        return carry

    lax.fori_loop(n_c, n_all, fill, 0)


def _select(scores, k_top, sub, rows_per_batch, out_dtype, tr=128):
    m, n_cols = scores.shape
    kern = functools.partial(_select_kernel, tr=tr, sub=sub, n_cols=n_cols, k_top=k_top,
                             rows_per_batch=rows_per_batch)
    return pl.pallas_call(
        kern,
        grid=(m // tr,),
        in_specs=[pl.BlockSpec((tr, n_cols), lambda i: (i, 0))],
        out_specs=pl.BlockSpec((tr, n_cols), lambda i: (i, 0)),
        out_shape=jax.ShapeDtypeStruct((m, n_cols), out_dtype),
        compiler_params=_cparams(1),
        name="select",
    )(scores)


def _attn_prompt_kernel(q_ref, k_ref, v_ref, b_ref, o_ref, m_ref, l_ref, acc_ref, *, tq, ts):
    i, j = pl.program_id(1), pl.program_id(2)
    n_rep = ts // LANES

    @pl.when(j == 0)
    def _():
        m_ref[...] = jnp.full(m_ref.shape, NEG_BIG, F32)
        l_ref[...] = jnp.zeros(l_ref.shape, F32)
        acc_ref[...] = jnp.zeros(acc_ref.shape, F32)

    @pl.when(j * ts <= i * tq + tq - 1)
    def _():
        bias = b_ref[...].astype(F32)
        nt = (((1,), (1,)), ((), ()))
        for h in range(H_A):
            sl = slice(h * DH_A, (h + 1) * DH_A)
            s = lax.dot_general(q_ref[:, sl], k_ref[:, sl], nt, preferred_element_type=F32) + bias
            m_prev = m_ref[h]
            m_next = jnp.maximum(m_prev, jnp.max(s, axis=1, keepdims=True))
            alpha = jnp.exp2(m_prev - m_next)
            p = jnp.exp2(s - jnp.tile(m_next, (1, n_rep)))
            l_ref[h] = alpha * l_ref[h] + jnp.sum(p, axis=1, keepdims=True)
            m_ref[h] = m_next
            pv = jnp.dot(p.astype(BF16), v_ref[:, sl], preferred_element_type=F32)
            acc_ref[:, sl] = acc_ref[:, sl] * alpha + pv

    @pl.when(j == pl.num_programs(2) - 1)
    def _():
        for h in range(H_A):
            sl = slice(h * DH_A, (h + 1) * DH_A)
            o_ref[:, sl] = (acc_ref[:, sl] / l_ref[h]).astype(o_ref.dtype)


def _attn_prompt(q_bf, k_bf, v_bf, bias, n_batch, t_len, tq=256, ts=512):
    nq, ns = t_len // tq, t_len // ts
    last = lambda i: (i * tq + tq - 1) // ts
    kern = functools.partial(_attn_prompt_kernel, tq=tq, ts=ts)
    return pl.pallas_call(
        kern,
        grid=(n_batch, nq, ns),
        in_specs=[pl.BlockSpec((tq, WA), lambda b, i, j: (b * nq + i, 0)),
                  pl.BlockSpec((ts, WA), lambda b, i, j: (b * ns + jnp.minimum(j, last(i)), 0)),
                  pl.BlockSpec((ts, WA), lambda b, i, j: (b * ns + jnp.minimum(j, last(i)), 0)),
                  pl.BlockSpec((tq, ts), lambda b, i, j: (b * nq + i, jnp.minimum(j, last(i))))],
        out_specs=pl.BlockSpec((tq, WA), lambda b, i, j: (b * nq + i, 0)),
        out_shape=jax.ShapeDtypeStruct((n_batch * t_len, WA), F32),
        scratch_shapes=[pltpu.VMEM((H_A, tq, LANES), F32), pltpu.VMEM((H_A, tq, LANES), F32),
                        pltpu.VMEM((tq, WA), F32)],
        compiler_params=_cparams(3),
        name="attn_prompt",
    )(q_bf, k_bf, v_bf, bias)


def _log_sigmoid(x):
    return jnp.minimum(x, 0.0) - jnp.log1p(jnp.exp(-jnp.abs(x)))


def _bf16_split3(x):
    x1 = x.astype(BF16)
    r1 = x - x1.astype(F32)
    x2 = r1.astype(BF16)
    x3 = (r1 - x2.astype(F32)).astype(BF16)
    return x1, x2, x3


def _gla_kernel(q_ref, k_ref, v_ref, tail_ref, wg_ref, bg_ref, s0_ref, o_ref, sfin_ref,
                st_ref, cum_all, kk_all, xs_all, *, tb, chunk, sub):
    c_id = pl.program_id(1)
    n_sub = chunk // sub
    nt = (((1,), (1,)), ((), ()))
    tn = (((0,), (0,)), ((), ()))

    @pl.when(c_id == 0)
    def _():
        for h in range(H_B):
            st_ref[h] = s0_ref[h].T

    rows = lax.broadcasted_iota(I32, (chunk, LANES), 0)
    cols = lax.broadcasted_iota(I32, (chunk, LANES), 1)
    tri = (lax.broadcasted_iota(I32, (chunk, chunk), 1)
           <= lax.broadcasted_iota(I32, (chunk, chunk), 0)).astype(BF16)
    ones = jnp.ones((LANES, LANES), BF16)

    def load(ref, r0, n_rows, c0, width):
        x = ref[pl.ds(r0, n_rows), c0:c0 + width]
        if n_rows < chunk:
            x = jnp.concatenate([x, jnp.zeros((chunk - n_rows, width), x.dtype)], axis=0)
        return x

    n_rows = min(tb, chunk)
    n_chunks = max(tb // chunk, 1)
    items = [(c, h) for c in range(n_chunks) for h in range(H_B)]
    slot_of = {it: i for i, it in enumerate(items)}
    lane_hit = [(cols % sub == sl) & (cols // sub == rows // sub) & (cols < chunk)
                for sl in range(sub)]
    row_blk = rows // sub

    ab = [load(tail_ref, c * chunk, n_rows, 0, N_TAIL)[:, L_AB:L_AB + GATE_RANK].astype(BF16)
          for c in range(n_chunks)]
    wg = [wg_ref[:, h * DK_B:(h + 1) * DK_B].astype(BF16) for h in range(H_B)]
    q, k, vb, la = {}, {}, {}, {}
    for c, h in items:
        q[c, h] = load(q_ref, c * chunk, n_rows, h * DK_B, DK_B) * (DK_B ** -0.5)
        k[c, h] = load(k_ref, c * chunk, n_rows, h * DK_B, DK_B)
        vb[c, h] = load(v_ref, c * chunk, n_rows, h * DV_B, DV_B).astype(BF16)
        x = jnp.dot(ab[c], wg[h], preferred_element_type=F32) + bg_ref[:, h * DK_B:(h + 1) * DK_B]
        la[c, h] = _log_sigmoid(x) / GATE_TAU
        if n_rows < chunk:
            la[c, h] = jnp.where(rows < n_rows, la[c, h], 0.0)

    cum = {}
    for it in items:
        l1, l2, l3 = _bf16_split3(la[it])
        cum[it] = (jnp.dot(tri, l1, preferred_element_type=F32)
                   + jnp.dot(tri, l2, preferred_element_type=F32)
                   + jnp.dot(tri, l3, preferred_element_type=F32))
        cum_all[slot_of[it]] = cum[it]
        kk_all[slot_of[it]] = k[it]

    att = {}
    for it in items:
        cum_ref = cum_all.at[slot_of[it]]
        a = jnp.zeros((chunk, LANES), F32)
        for i in range(1, n_sub):
            r_i = cum_ref[i * sub - 1:i * sub, :]
            qt = (q[it] * jnp.exp(jnp.minimum(cum[it] - r_i, 0.0))).astype(BF16)
            kt = (k[it] * jnp.exp(jnp.minimum(r_i - cum[it], 0.0))).astype(BF16)
            a_i = lax.dot_general(qt, kt, nt, preferred_element_type=F32)
            if chunk < LANES:
                a_i = jnp.concatenate([a_i, jnp.zeros((chunk, LANES - chunk), F32)], axis=1)
            a = jnp.where((row_blk == i) & (cols < i * sub), a_i, a)
        att[it] = a

    for it in items:
        cum_ref, kk_ref, xs_ref = (r.at[slot_of[it]] for r in (cum_all, kk_all, xs_all))
        for sl in range(sub):
            cum_s = jnp.concatenate(
                [jnp.broadcast_to(cum_ref[i * sub + sl:i * sub + sl + 1, :], (sub, LANES))
                 for i in range(n_sub)], axis=0)
            k_s = jnp.concatenate(
                [jnp.broadcast_to(kk_ref[i * sub + sl:i * sub + sl + 1, :], (sub, LANES))
                 for i in range(n_sub)], axis=0)
            dec = jnp.where(rows % sub >= sl, jnp.exp(jnp.minimum(cum[it] - cum_s, 0.0)), 0.0)
            xs_ref[sl * chunk:(sl + 1) * chunk, :] = (q[it] * k_s * dec).astype(BF16)
    n_x = sub * chunk
    red = jnp.dot(xs_all[...].reshape(len(items) * n_x, LANES), ones, preferred_element_type=F32)

    o_intra, upd, qdec, sdec = {}, {}, {}, {}
    for it in items:
        a = att[it]
        base = slot_of[it] * n_x
        for sl in range(sub):
            a = jnp.where(lane_hit[sl], red[base + sl * chunk:base + (sl + 1) * chunk, :], a)
        o_intra[it] = jnp.dot(a[:, :chunk].astype(BF16), vb[it], preferred_element_type=F32)
        last = cum_all[slot_of[it], chunk - 1:chunk, :]
        kd = (k[it] * jnp.exp(last - cum[it])).astype(BF16)
        upd[it] = lax.dot_general(vb[it], kd, tn, preferred_element_type=F32)
        qdec[it] = (q[it] * jnp.exp(cum[it])).astype(BF16)
        sdec[it] = jnp.exp(last)

    for h in range(H_B):
        st = st_ref[h]
        for c in range(n_chunks):
            o = o_intra[c, h] + lax.dot_general(qdec[c, h], st.astype(BF16), nt,
                                                preferred_element_type=F32)
            o_ref[c * chunk:c * chunk + n_rows, h * DV_B:(h + 1) * DV_B] = o[:n_rows]
            st = st * sdec[c, h] + upd[c, h]
        st_ref[h] = st

    @pl.when(c_id == pl.num_programs(1) - 1)
    def _():
        for h in range(H_B):
            sfin_ref[h] = st_ref[h].T


def _gla(z, tail, w_gate_up, b_gate, s0, n_batch, t_len, tb, chunk, sub):
    nb = t_len // tb
    n_slots = max(tb // chunk, 1) * H_B
    kern = functools.partial(_gla_kernel, tb=tb, chunk=chunk, sub=sub)
    return pl.pallas_call(
        kern,
        grid=(n_batch, nb),
        in_specs=[pl.BlockSpec((tb, WKB), lambda b, c: (b * nb + c, C_QB // WKB)),
                  pl.BlockSpec((tb, WKB), lambda b, c: (b * nb + c, C_KB // WKB)),
                  pl.BlockSpec((tb, WVB), lambda b, c: (b * nb + c, C_VB // WVB)),
                  pl.BlockSpec((tb, N_TAIL), lambda b, c: (b * nb + c, 0)),
                  pl.BlockSpec((GATE_RANK, WKB), lambda b, c: (0, 0)),
                  pl.BlockSpec((1, WKB), lambda b, c: (0, 0)),
                  pl.BlockSpec((None, H_B, DK_B, DV_B), lambda b, c: (b, 0, 0, 0))],
        out_specs=[pl.BlockSpec((tb, WVB), lambda b, c: (b * nb + c, 0)),
                   pl.BlockSpec((None, H_B, DK_B, DV_B), lambda b, c: (b, 0, 0, 0))],
        out_shape=[jax.ShapeDtypeStruct((n_batch * t_len, WVB), F32),
                   jax.ShapeDtypeStruct((n_batch, H_B, DK_B, DV_B), F32)],
        scratch_shapes=[pltpu.VMEM((H_B, DV_B, DK_B), F32), pltpu.VMEM((n_slots, chunk, LANES), F32),
                        pltpu.VMEM((n_slots, chunk, LANES), F32),
                        pltpu.VMEM((n_slots, sub * chunk, LANES), BF16)],
        compiler_params=_cparams(2),
        name="gla",
    )(z, z, z, tail, w_gate_up, b_gate, s0)


def _silu(x):
    return x / (1.0 + jnp.exp(-x))


def _merge_kernel(att_ref, ga_ref, ob_ref, gb_ref, gn_ref, wo_ref, x_ref, nf_ref, o_ref):
    a = (att_ref[...] * _silu(ga_ref[...])).astype(BF16)
    y = jnp.dot(a, wo_ref[:WA, :], preferred_element_type=F32)
    gn = gn_ref[...]
    for h in range(H_B):
        sl = slice(h * DV_B, (h + 1) * DV_B)
        ob = ob_ref[:, sl]
        bn = ob * lax.rsqrt(jnp.mean(ob * ob, axis=-1, keepdims=True) + EPS) * gn
        bp = (bn * _silu(gb_ref[:, sl])).astype(BF16)
        y = y + jnp.dot(bp, wo_ref[WA + h * DV_B:WA + (h + 1) * DV_B, :], preferred_element_type=F32)
    xo = x_ref[...] + y
    o_ref[...] = xo * lax.rsqrt(jnp.mean(xo * xo, axis=-1, keepdims=True) + EPS) * nf_ref[...]


def _merge(att, z, ob, gla_norm, w_out_bf, x2d, norm_f, tm):
    m = x2d.shape[0]
    return pl.pallas_call(
        _merge_kernel,
        grid=(m // tm,),
        in_specs=[pl.BlockSpec((tm, WA), lambda i: (i, 0)),
                  pl.BlockSpec((tm, WA), lambda i: (i, C_GA // WA)),
                  pl.BlockSpec((tm, WVB), lambda i: (i, 0)),
                  pl.BlockSpec((tm, WVB), lambda i: (i, C_GB // WVB)),
                  pl.BlockSpec((1, DV_B), lambda i: (0, 0)),
                  pl.BlockSpec((WA + WVB, D_MODEL), lambda i: (0, 0)),
                  pl.BlockSpec((tm, D_MODEL), lambda i: (i, 0)),
                  pl.BlockSpec((1, D_MODEL), lambda i: (0, 0))],
        out_specs=pl.BlockSpec((tm, D_MODEL), lambda i: (i, 0)),
        out_shape=jax.ShapeDtypeStruct((m, D_MODEL), F32),
        compiler_params=_cparams(1),
        name="merge",
    )(att, z, ob, z, gla_norm, w_out_bf, x2d, norm_f)


PAGES_PER_STEP = 16


def _idx_pages_kernel(pt_ref, q_ref, w_ref, *refs, n_pages, t_dec, causal):
    page_refs, o_ref = refs[:n_pages], refs[n_pages]
    q = q_ref[...]
    w = w_ref[...]
    for p in range(n_pages):
        kp = page_refs[p][...].astype(BF16)
        d = jnp.dot(q, kp, preferred_element_type=F32)
        r = w * jnp.maximum(d, 0.0)
        acc = r[0:t_dec]
        for h in range(1, H_IDX):
            acc = acc + r[h * t_dec:(h + 1) * t_dec]
        if causal:
            t_i = lax.broadcasted_iota(I32, acc.shape, 0)
            s_i = lax.broadcasted_iota(I32, acc.shape, 1)
            acc = jnp.where(s_i <= t_i, acc, -jnp.inf)
        o_ref[:, p * PAGE_SIZE:(p + 1) * PAGE_SIZE] = acc


def _idx_sample_past(page_table, q_stk, w_stk, pool_ik, n_seq, t_dec, n_pages):
    pps = PAGES_PER_STEP
    n_g = n_pages // pps
    rows = H_IDX * t_dec
    kern = functools.partial(_idx_pages_kernel, n_pages=pps, t_dec=t_dec, causal=False)
    page_spec = lambda r: pl.BlockSpec(
        (None, D_IDX, PAGE_SIZE), lambda b, g, pt, r=r: (pt[b * n_pages + g * pps + r], 0, 0))
    return pl.pallas_call(
        kern,
        grid_spec=pltpu.PrefetchScalarGridSpec(
            num_scalar_prefetch=1,
            grid=(n_seq, n_g),
            in_specs=[pl.BlockSpec((rows, D_IDX), lambda b, g, pt: (b, 0)),
                      pl.BlockSpec((rows, 1), lambda b, g, pt: (b, 0))]
                     + [page_spec(r) for r in range(pps)],
            out_specs=pl.BlockSpec((t_dec, pps * PAGE_SIZE), lambda b, g, pt: (b, g))),
        out_shape=jax.ShapeDtypeStruct((n_seq * t_dec, n_pages * PAGE_SIZE), F32),
        compiler_params=_cparams(2),
        name="idx_sample_past",
    )(page_table.reshape(-1), q_stk, w_stk, *([pool_ik] * pps))


def _idx_sample_new(q_stk, w_stk, ki_new_pad, n_seq, t_dec):
    rows = H_IDX * t_dec
    kern = functools.partial(_idx_pages_kernel, None, n_pages=1, t_dec=t_dec, causal=True)
    return pl.pallas_call(
        kern,
        grid=(n_seq,),
        in_specs=[pl.BlockSpec((rows, D_IDX), lambda b: (b, 0)),
                  pl.BlockSpec((rows, 1), lambda b: (b, 0)),
                  pl.BlockSpec((None, D_IDX, PAGE_SIZE), lambda b: (b, 0, 0))],
        out_specs=pl.BlockSpec((t_dec, PAGE_SIZE), lambda b: (b, 0)),
        out_shape=jax.ShapeDtypeStruct((n_seq * t_dec, PAGE_SIZE), F32),
        compiler_params=_cparams(1),
        name="idx_sample_new",
    )(q_stk, w_stk, ki_new_pad)


ATTN_PAGES_PER_STEP = 8


def _attn_sample_kernel(pt_ref, q_ref, sel_ref, selnew_ref, e_ref, knew_ref, vnew_ref, *refs,
                        n_pages, t_dec):
    k_refs, v_refs = refs[:n_pages], refs[n_pages:2 * n_pages]
    o_ref, m_ref, l_ref, acc_ref = refs[2 * n_pages:]
    g = pl.program_id(1)
    rows = H_A * t_dec
    nt = (((1,), (1,)), ((), ()))
    q = q_ref[...]
    e = e_ref[...]
    n_col = PAGE_SIZE * H_A
    head_ok = (lax.broadcasted_iota(I32, (rows, n_col), 0) // t_dec
               == lax.broadcasted_iota(I32, (rows, n_col), 1) % H_A)

    def update(sels, k_pages, v_pages):
        scores = []
        m_next = m_ref[...]
        for sel, k_page in zip(sels, k_pages):
            picked = jnp.where(jnp.tile(sel, (H_A, 1)) == 0.0, 1.0, 0.0).astype(BF16)
            picked = jnp.dot(picked, e, preferred_element_type=F32)
            s = lax.dot_general(q, k_page[...].astype(BF16), nt, preferred_element_type=F32)
            s = jnp.where(head_ok, jnp.where(picked > 0.5, s, NEG_BIG), NEG_BIG)
            m_next = jnp.maximum(m_next, jnp.max(s, axis=1, keepdims=True))
            scores.append(s)
        alpha = jnp.exp2(m_ref[...] - m_next)
        m_rep = jnp.tile(m_next, (1, n_col // LANES))
        l_new = alpha * l_ref[...]
        acc = acc_ref[...] * alpha
        for s, v_page in zip(scores, v_pages):
            p = jnp.exp2(s - m_rep)
            l_new = l_new + jnp.sum(p, axis=1, keepdims=True)
            acc = acc + jnp.dot(p.astype(BF16), v_page[...].astype(BF16), preferred_element_type=F32)
        m_ref[...] = m_next
        l_ref[...] = l_new
        acc_ref[...] = acc

    @pl.when(g == 0)
    def _():
        m_ref[...] = jnp.full(m_ref.shape, NEG_BIG, F32)
        l_ref[...] = jnp.zeros(l_ref.shape, F32)
        acc_ref[...] = jnp.zeros(acc_ref.shape, F32)
        update([selnew_ref[...]], [knew_ref], [vnew_ref])

    update([sel_ref[:, p_i * PAGE_SIZE:(p_i + 1) * PAGE_SIZE] for p_i in range(n_pages)],
           k_refs, v_refs)

    @pl.when(g == pl.num_programs(1) - 1)
    def _():
        o_ref[...] = acc_ref[...] / l_ref[...]


def _attn_sample(page_table, q_stk, sel_past, sel_new, expand, knew_pad, vnew_pad, pool_k, pool_v,
                 n_seq, t_dec, n_pages):
    pps = ATTN_PAGES_PER_STEP
    n_g = n_pages // pps
    rows = H_A * t_dec
    n_col = PAGE_SIZE * H_A
    kern = functools.partial(_attn_sample_kernel, n_pages=pps, t_dec=t_dec)
    page_spec = lambda r: pl.BlockSpec(
        (None, n_col, DH_A), lambda b, g, pt, r=r: (pt[b * n_pages + g * pps + r], 0, 0))
    return pl.pallas_call(
        kern,
        grid_spec=pltpu.PrefetchScalarGridSpec(
            num_scalar_prefetch=1,
            grid=(n_seq, n_g),
            in_specs=[pl.BlockSpec((rows, DH_A), lambda b, g, pt: (b, 0)),
                      pl.BlockSpec((t_dec, pps * PAGE_SIZE), lambda b, g, pt: (b, g)),
                      pl.BlockSpec((t_dec, PAGE_SIZE), lambda b, g, pt: (b, 0)),
                      pl.BlockSpec((PAGE_SIZE, n_col), lambda b, g, pt: (0, 0)),
                      pl.BlockSpec((None, n_col, DH_A), lambda b, g, pt: (b, 0, 0)),
                      pl.BlockSpec((None, n_col, DH_A), lambda b, g, pt: (b, 0, 0))]
                     + [page_spec(r) for r in range(pps)] * 2,
            out_specs=pl.BlockSpec((rows, DH_A), lambda b, g, pt: (b, 0)),
            scratch_shapes=[pltpu.VMEM((rows, LANES), F32), pltpu.VMEM((rows, LANES), F32),
                            pltpu.VMEM((rows, DH_A), F32)]),
        out_shape=jax.ShapeDtypeStruct((n_seq * rows, DH_A), F32),
        compiler_params=_cparams(2),
        name="attn_sample",
    )(page_table.reshape(-1), q_stk, sel_past, sel_new, expand, knew_pad, vnew_pad,
      *([pool_k] * pps), *([pool_v] * pps))


def _group_common(x, norm_in, w_parts, pos, tm, tn, tr):
    n_seq, t_len, _ = x.shape
    x2d = x.reshape(n_seq * t_len, D_MODEL)
    w_full, w_gla, w_tail = w_parts
    z = _proj(x2d, norm_in, w_full, C_QB, w_gla, tm, tn)
    tail = _proj(x2d, norm_in, w_tail, N_TAIL, None, tm, N_TAIL)
    tab = _rope_tables(pos)
    return x2d, z, tail, _rope(z, tail, tab, tr)


def kernel(x_prompt, x_sample, cache_k, cache_v, cache_idx_k, state_gla, page_table,
           norm_in, w_in, w_gate_up, b_gate, gla_norm, w_out, norm_f):
    n_p, t_p, _ = x_prompt.shape
    n_s, t_s, _ = x_sample.shape
    n_pages = page_table.shape[1]
    past = n_pages * PAGE_SIZE
    assert cache_k.shape[0] == 1, "single layer"

    w_parts = _prep_w_in(w_in[0])
    w_out_bf = w_out[0].astype(BF16)
    g_in, g_gla, g_f = norm_in[0][None], gla_norm[0][None], norm_f[None]
    wg, bg = w_gate_up[0], b_gate[0][None]

    xp2d, zp, tailp, (newk_p, newv_p, newik_p, q_p, k_p, v_p, qi_p, kie_p, kio_p) = _group_common(
        x_prompt, g_in, w_parts, jnp.arange(t_p), 1024, 512, 256)
    sc_p = _idx_prompt(qi_p, tailp, kie_p, kio_p, n_p, t_p)
    bias_p = _select(sc_p, min(TOPK_MAX, t_p // 4), 4, t_p, BF16)
    att_p = _attn_prompt(q_p, k_p, v_p, bias_p, n_p, t_p)
    s0_p = jnp.zeros((n_p, H_B, DK_B, DV_B), F32)
    ob_p, sfin_p = _gla(zp, tailp, wg, bg, s0_p, n_p, t_p, 256, 64, 16)
    y_p = _merge(att_p, zp, ob_p, g_gla, w_out_bf, xp2d, g_f, 256)

    m_s = n_s * t_s
    pos_s = jnp.tile(past + jnp.arange(t_s), n_s)
    xs2d, zs, tails, (newk_s, newv_s, newik_s, q_s, k_s, v_s, qi_s, kie_s, kio_s) = _group_common(
        x_sample, g_in, w_parts, pos_s, m_s, 512, m_s)
    qi_stk = qi_s.reshape(n_s, t_s, H_IDX, D_IDX).transpose(0, 2, 1, 3).reshape(n_s * H_IDX * t_s, D_IDX)
    w_stk = (tails[:, L_WI:L_WI + H_IDX] * IDX_W_SCALE).reshape(n_s, t_s, H_IDX).transpose(0, 2, 1)
    w_stk = w_stk.reshape(n_s * H_IDX * t_s, 1)
    q_stk = q_s.reshape(n_s, t_s, H_A, DH_A).transpose(0, 2, 1, 3).reshape(n_s * H_A * t_s, DH_A)
    pad_rows = lambda a, n: jnp.pad(a, ((0, 0), (0, n - a.shape[1]), (0, 0)))
    ki_new_pad = jnp.pad(newik_s.reshape(n_s, t_s, D_IDX).transpose(0, 2, 1),
                         ((0, 0), (0, 0), (0, PAGE_SIZE - t_s)))
    knew_pad = pad_rows(newk_s.reshape(n_s, t_s * H_A, DH_A), PAGE_SIZE * H_A)
    vnew_pad = pad_rows(newv_s.reshape(n_s, t_s * H_A, DH_A), PAGE_SIZE * H_A)
    pool_ik = jnp.swapaxes(cache_idx_k[0], 1, 2)
    pool_k = cache_k[0].reshape(-1, PAGE_SIZE * H_A, DH_A)
    pool_v = cache_v[0].reshape(-1, PAGE_SIZE * H_A, DH_A)

    sc_past = _idx_sample_past(page_table, qi_stk, w_stk, pool_ik, n_s, t_s, n_pages)
    sc_new = _idx_sample_new(qi_stk, w_stk, ki_new_pad, n_s, t_s)
    sc_s = jnp.concatenate([sc_past, sc_new], axis=1)
    k_top_s = min(TOPK_MAX, (past + t_s) // 4)
    n_tiles_s = sc_s.shape[1] // LANES
    sub_s = max(d for d in range(1, 6) if n_tiles_s % d == 0)
    bias_s = _select(sc_s, k_top_s, sub_s, 0, F32)
    expand = (jnp.arange(PAGE_SIZE * H_A)[None, :] // H_A == jnp.arange(PAGE_SIZE)[:, None]).astype(BF16)
    att_stk = _attn_sample(page_table, q_stk, bias_s[:, :past], bias_s[:, past:], expand,
                           knew_pad, vnew_pad, pool_k, pool_v, n_s, t_s, n_pages)
    att_s = att_stk.reshape(n_s, H_A, t_s, DH_A).transpose(0, 2, 1, 3).reshape(m_s, WA)
    ob_s, sfin_s = _gla(zs, tails, wg, bg, state_gla[0], n_s, t_s, t_s, 16, 16)
    y_s = _merge(att_s, zs, ob_s, g_gla, w_out_bf, xs2d, g_f, m_s)

    return (y_p.reshape(n_p, t_p, D_MODEL), y_s.reshape(n_s, t_s, D_MODEL),
            newk_p.reshape(1, n_p, t_p, H_A, DH_A), newv_p.reshape(1, n_p, t_p, H_A, DH_A),
            newik_p.reshape(1, n_p, t_p, D_IDX), sfin_p[None],
            newk_s.reshape(1, n_s, t_s, H_A, DH_A), newv_s.reshape(1, n_s, t_s, H_A, DH_A),
            newik_s.reshape(1, n_s, t_s, D_IDX), sfin_s[None])
```

```python
import functools
import math

import jax
import jax.numpy as jnp
import numpy as np
from jax import lax
from jax.experimental import pallas as pl
from jax.experimental.pallas import tpu as pltpu

F32 = jnp.float32
BF16 = jnp.bfloat16
I32 = jnp.int32

D_MODEL = 2048
PAGE_SIZE = 128
DH_A = 128
H_A = 8
H_IDX = 16
D_IDX = 64
TOPK_MAX = 256
DK_B = 128
DV_B = 256
H_B = 4
GATE_RANK = 16
GATE_TAU = 16.0
ROPE_THETA = 10000.0
EPS = 1e-6
NEG_BIG = -1e30
WA = H_A * DH_A
WI = H_IDX * D_IDX
WKB = H_B * DK_B
WVB = H_B * DV_B

C_QA, C_KA, C_VA, C_GA, C_QI = 0, WA, 2 * WA, 3 * WA, 4 * WA
C_QB = C_QI + WI
C_KB = C_QB + WKB
C_VB = C_KB + WKB
C_GB = C_VB + WVB
N_MAIN = C_GB + WVB
L_KI, L_WI, L_AB = 0, D_IDX, D_IDX + H_IDX
N_TAIL = 128

LANES = 128
INT_MIN = -2 ** 31
VMEM_LIMIT = 48 * 1024 * 1024


def _prep_w_in(w_in):
    o_ki = C_QI + WI
    o_qb = o_ki + D_IDX + H_IDX
    o_ab = o_qb + 2 * WKB + 2 * WVB
    w_bf = w_in.astype(BF16)
    pad = jnp.zeros((w_in.shape[0], N_TAIL - D_IDX - H_IDX - GATE_RANK), BF16)
    tail = jnp.concatenate([w_bf[:, o_ki:o_qb], w_bf[:, o_ab:], pad], axis=1)
    return w_bf, w_bf[:, o_qb:o_ab], tail


def _cparams(n_axes):
    return pltpu.CompilerParams(dimension_semantics=("arbitrary",) * n_axes,
                                vmem_limit_bytes=VMEM_LIMIT)


def _proj_kernel(x_ref, g_ref, wa_ref, wb_ref, o_ref, h_ref, *, n_a):
    j = pl.program_id(1)

    @pl.when(j == 0)
    def _():
        x = x_ref[...]
        ms = jnp.mean(x * x, axis=-1, keepdims=True)
        h_ref[...] = (x * lax.rsqrt(ms + EPS) * g_ref[...]).astype(BF16)

    @pl.when(j < n_a)
    def _():
        o_ref[...] = jnp.dot(h_ref[...], wa_ref[...], preferred_element_type=F32)

    @pl.when(j >= n_a)
    def _():
        o_ref[...] = jnp.dot(h_ref[...], wb_ref[...], preferred_element_type=F32)


def _proj(x2d, g, w_a, n_cols_a, w_b, tm, tn):
    m, k = x2d.shape
    n_a = n_cols_a // tn
    if w_b is None:
        w_b, n_b = w_a, 0
    else:
        n_b = w_b.shape[1] // tn
    return pl.pallas_call(
        functools.partial(_proj_kernel, n_a=n_a),
        grid=(m // tm, n_a + n_b),
        in_specs=[pl.BlockSpec((tm, k), lambda i, j: (i, 0)),
                  pl.BlockSpec((1, k), lambda i, j: (0, 0)),
                  pl.BlockSpec((k, tn), lambda i, j: (0, jnp.minimum(j, n_a - 1))),
                  pl.BlockSpec((k, tn), lambda i, j: (0, jnp.maximum(j - n_a, 0)))],
        out_specs=pl.BlockSpec((tm, tn), lambda i, j: (i, j)),
        out_shape=jax.ShapeDtypeStruct((m, (n_a + n_b) * tn), F32),
        scratch_shapes=[pltpu.VMEM((tm, k), BF16)],
        compiler_params=_cparams(2),
        name="proj",
    )(x2d, g, w_a, w_b)


def _rope128(x, cos, sin_signed):
    return x * cos + pltpu.roll(x, 64, axis=1) * sin_signed


def _rope64(x, cos, sin_lo, sin_hi):
    return x * cos + pltpu.roll(x, 96, axis=1) * sin_lo + pltpu.roll(x, 32, axis=1) * sin_hi


def _rope_kernel(qa_ref, ka_ref, va_ref, qi_ref, tail_ref, tab_ref,
                 newk_ref, newv_ref, newik_ref, qbf_ref, kbf_ref, vbf_ref, qibf_ref,
                 kie_ref, kio_ref):
    cos_a, sin_a = tab_ref[0], tab_ref[1]
    cos_i, sin_lo, sin_hi = tab_ref[2], tab_ref[3], tab_ref[4]
    scale = (DH_A ** -0.5) * math.log2(math.e)
    for h in range(H_A):
        sl = slice(h * DH_A, (h + 1) * DH_A)
        q = _rope128(qa_ref[:, sl], cos_a, sin_a)
        qbf_ref[:, sl] = (q * scale).astype(BF16)
        k = _rope128(ka_ref[:, sl], cos_a, sin_a)
        newk_ref[:, sl] = k
        kbf_ref[:, sl] = k.astype(BF16)
    v = va_ref[...]
    newv_ref[...] = v
    ones = jnp.ones((v.shape[0], DH_A), BF16)
    for h in range(H_A):
        vbf_ref[:, 2 * h * DH_A:(2 * h + 1) * DH_A] = v[:, h * DH_A:(h + 1) * DH_A].astype(BF16)
        vbf_ref[:, (2 * h + 1) * DH_A:(2 * h + 2) * DH_A] = ones
    for j in range(WI // LANES):
        sl = slice(j * LANES, (j + 1) * LANES)
        qibf_ref[:, sl] = _rope64(qi_ref[:, sl], cos_i, sin_lo, sin_hi).astype(BF16)
    ki = _rope64(tail_ref[...], cos_i, sin_lo, sin_hi)
    newik_ref[...] = ki[:, L_KI:L_KI + D_IDX]
    lane = lax.broadcasted_iota(I32, ki.shape, 1)
    ki_lo = jnp.where(lane < D_IDX, ki, 0.0)
    kie_ref[...] = ki_lo.astype(BF16)
    kio_ref[...] = pltpu.roll(ki_lo, D_IDX, axis=1).astype(BF16)


def _rope(z, tail, tab, tm):
    m = z.shape[0]
    n_tab = tab.shape[1] // tm
    col = lambda c: pl.BlockSpec((tm, WA), lambda i, c=c: (i, c))
    row = lambda w: pl.BlockSpec((tm, w), lambda i: (i, 0))
    return pl.pallas_call(
        _rope_kernel,
        grid=(m // tm,),
        in_specs=[col(C_QA // WA), col(C_KA // WA), col(C_VA // WA), col(C_QI // WA),
                  row(N_TAIL),
                  pl.BlockSpec((5, tm, LANES), lambda i: (0, i % n_tab, 0))],
        out_specs=[row(WA), row(WA), row(D_IDX), row(WA), row(WA), row(2 * WA), row(WI),
                   row(LANES), row(LANES)],
        out_shape=[jax.ShapeDtypeStruct((m, WA), F32), jax.ShapeDtypeStruct((m, WA), F32),
                   jax.ShapeDtypeStruct((m, D_IDX), F32),
                   jax.ShapeDtypeStruct((m, WA), BF16), jax.ShapeDtypeStruct((m, WA), BF16),
                   jax.ShapeDtypeStruct((m, 2 * WA), BF16), jax.ShapeDtypeStruct((m, WI), BF16),
                   jax.ShapeDtypeStruct((m, LANES), BF16), jax.ShapeDtypeStruct((m, LANES), BF16)],
        compiler_params=_cparams(1),
        name="rope",
    )(z, z, z, z, tail, tab)


def _rope_tables(pos):
    pos = pos.astype(F32)[:, None]
    half_a, half_i = DH_A // 2, D_IDX // 2
    inv_a = ROPE_THETA ** (-jnp.arange(half_a, dtype=F32) / half_a)
    inv_i = ROPE_THETA ** (-jnp.arange(half_i, dtype=F32) / half_i)
    ca, sa = jnp.cos(pos * inv_a), jnp.sin(pos * inv_a)
    ci, si = jnp.cos(pos * inv_i), jnp.sin(pos * inv_i)
    zi = jnp.zeros_like(si)
    return jnp.stack([
        jnp.concatenate([ca, ca], axis=1),
        jnp.concatenate([-sa, sa], axis=1),
        jnp.concatenate([ci, ci, ci, ci], axis=1),
        jnp.concatenate([-si, zi, -si, zi], axis=1),
        jnp.concatenate([zi, si, zi, si], axis=1)])


IDX_W_SCALE = (H_IDX ** -0.5) * (D_IDX ** -0.5)


def _idx_scores(q_ref, w, kie, kio):
    nt = (((1,), (1,)), ((), ()))
    acc = jnp.zeros((q_ref.shape[0], kie.shape[0]), F32)
    for j in range(H_IDX // 2):
        qp = q_ref[:, j * LANES:(j + 1) * LANES]
        de = lax.dot_general(qp, kie, nt, preferred_element_type=F32)
        acc = acc + w[:, 2 * j:2 * j + 1] * jnp.maximum(de, 0.0)
        do = lax.dot_general(qp, kio, nt, preferred_element_type=F32)
        acc = acc + w[:, 2 * j + 1:2 * j + 2] * jnp.maximum(do, 0.0)
    return acc


def _idx_prompt_kernel(q_ref, tail_ref, kie_ref, kio_ref, o_ref, *, tq, tc, t_len):
    i = pl.program_id(1)
    w = tail_ref[:, L_WI:L_WI + H_IDX] * IDX_W_SCALE
    n_c = (i * tq + tq - 1) // tc + 1
    row = i * tq + lax.broadcasted_iota(I32, (tq, tc), 0)

    def body(c, carry):
        c0 = pl.multiple_of(c * tc, tc)
        acc = _idx_scores(q_ref, w, kie_ref[pl.ds(c0, tc), :], kio_ref[pl.ds(c0, tc), :])
        col = c0 + lax.broadcasted_iota(I32, (tq, tc), 1)
        o_ref[:, pl.ds(c0, tc)] = jnp.where(col <= row, acc, -jnp.inf)
        return carry

    lax.fori_loop(0, n_c, body, 0)

    def fill(c, carry):
        c0 = pl.multiple_of(c * tc, tc)
        o_ref[:, pl.ds(c0, tc)] = jnp.full((tq, tc), -jnp.inf, F32)
        return carry

    lax.fori_loop(n_c, t_len // tc, fill, 0)


def _idx_prompt(qi_bf, tail, kie, kio, n_batch, t_len, tq=128, tc=256):
    nq = t_len // tq
    kern = functools.partial(_idx_prompt_kernel, tq=tq, tc=tc, t_len=t_len)
    return pl.pallas_call(
        kern,
        grid=(n_batch, nq),
        in_specs=[pl.BlockSpec((tq, WI), lambda b, i: (b * nq + i, 0)),
                  pl.BlockSpec((tq, N_TAIL), lambda b, i: (b * nq + i, 0)),
                  pl.BlockSpec((t_len, LANES), lambda b, i: (b, 0)),
                  pl.BlockSpec((t_len, LANES), lambda b, i: (b, 0))],
        out_specs=pl.BlockSpec((tq, t_len), lambda b, i: (b * nq + i, 0)),
        out_shape=jax.ShapeDtypeStruct((n_batch * t_len, t_len), F32),
        compiler_params=_cparams(2),
        name="idx_prompt",
    )(qi_bf, tail, kie, kio)


SELECT_STEPS_PER_CHECK = 3
SELECT_MAX_CHECKS = 16
SELECT_ROWS = 128


def _select_body(s_ref, o_ref, *, n_half, sub, n_all, n_c, unroll, k_top):
    tc = sub * LANES
    kf = float(k_top)
    halves = [pl.ds(hh * SELECT_ROWS, SELECT_ROWS) for hh in range(n_half)]
    shape = (SELECT_ROWS, LANES)

    def over_chunks(fn, init, lo_c, hi_c):
        if unroll:
            carry = init
            for c in range(lo_c, hi_c):
                carry = fn(c * tc, carry)
            return carry
        return lax.fori_loop(lo_c, hi_c, lambda c, carry: fn(pl.multiple_of(c * tc, tc), carry), init)

    def stats(c0, carry):
        out = []
        for rows, (mn, mx, cnt) in zip(halves, carry):
            for u in range(sub):
                x = s_ref[rows, pl.ds(c0 + u * LANES, LANES)]
                valid = x > -jnp.inf
                mn = jnp.minimum(mn, jnp.where(valid, x, jnp.inf))
                mx = jnp.maximum(mx, x)
                cnt = cnt + jnp.where(valid, 1.0, 0.0)
            out.append((mn, mx, cnt))
        return tuple(out)

    st = over_chunks(stats, tuple((jnp.full(shape, jnp.inf, F32), jnp.full(shape, -jnp.inf, F32),
                                   jnp.zeros(shape, F32)) for _ in halves), 0, n_c)
    lo0 = tuple(jnp.min(mn, axis=1, keepdims=True) for mn, _, _ in st)
    hi0 = tuple(jnp.max(mx, axis=1, keepdims=True) for _, mx, _ in st)
    c0_lo = tuple(jnp.sum(cnt, axis=1, keepdims=True) for _, _, cnt in st)

    def count_ge(thrs):
        thr_b = [jnp.broadcast_to(t, shape) for t in thrs]

        def body(c0, accs):
            out = []
            for rows, tb, acc in zip(halves, thr_b, accs):
                for u in range(sub):
                    x = s_ref[rows, pl.ds(c0 + u * LANES, LANES)]
                    acc = acc + jnp.where(x >= tb, 1.0, 0.0)
                out.append(acc)
            return tuple(out)

        accs = over_chunks(body, tuple(jnp.zeros(shape, F32) for _ in halves), 0, n_c)
        return [jnp.sum(a, axis=1, keepdims=True) for a in accs]

    def unresolved(state):
        it, _, _, c_lo = state
        worst = c_lo[0]
        for c in c_lo[1:]:
            worst = jnp.maximum(worst, c)
        return jnp.logical_and(it < SELECT_MAX_CHECKS, jnp.max(worst) > kf)

    def refine(state):
        it, lo, hi, c_lo = state
        lo, hi, c_lo = list(lo), list(hi), list(c_lo)
        for _ in range(SELECT_STEPS_PER_CHECK):
            mid = [0.5 * a + 0.5 * b for a, b in zip(lo, hi)]
            cnt = count_ge(mid)
            for i in range(n_half):
                ok = cnt[i] >= kf
                lo[i] = jnp.where(ok, mid[i], lo[i])
                c_lo[i] = jnp.where(ok, cnt[i], c_lo[i])
                hi[i] = jnp.where(ok, hi[i], mid[i])
        return it + 1, tuple(lo), tuple(hi), tuple(c_lo)

    _, thr, _, _ = lax.while_loop(unresolved, refine, (jnp.int32(0), lo0, hi0, c0_lo))

    def emit(c0, carry):
        for rows, t in zip(halves, thr):
            x = s_ref[rows, pl.ds(c0, tc)]
            o_ref[rows, pl.ds(c0, tc)] = jnp.where(x >= t, 0.0, NEG_BIG).astype(o_ref.dtype)
        return carry

    over_chunks(emit, 0, 0, n_c)

    def fill(c0, carry):
        o_ref[:, pl.ds(c0, tc)] = jnp.full((n_half * SELECT_ROWS, tc), NEG_BIG, o_ref.dtype)
        return carry

    over_chunks(fill, 0, n_c, n_all)


def _select_kernel(s_ref, o_ref, *, n_half, sub, n_cols, k_top, rows_per_batch, n_groups):
    tc = sub * LANES
    n_all = n_cols // tc
    body = functools.partial(_select_body, s_ref, o_ref, n_half=n_half, sub=sub, n_all=n_all, k_top=k_top)
    if rows_per_batch:
        tr = n_half * SELECT_ROWS
        r0 = (pl.program_id(0) % (rows_per_batch // tr)) * tr
        rows_per_group = rows_per_batch // n_groups
        for g in range(n_groups):
            @pl.when(r0 // rows_per_group == g)
            def _(g=g):
                body(n_c=((g + 1) * rows_per_group - 1) // tc + 1, unroll=True)
    else:
        body(n_c=n_all, unroll=False)


def _select(scores, k_top, sub, rows_per_batch, out_dtype, n_half=1, n_groups=1):
    m, n_cols = scores.shape
    tr = n_half * SELECT_ROWS
    kern = functools.partial(_select_kernel, n_half=n_half, sub=sub, n_cols=n_cols, k_top=k_top,
                             rows_per_batch=rows_per_batch, n_groups=n_groups)
    return pl.pallas_call(
        kern,
        grid=(m // tr,),
        in_specs=[pl.BlockSpec((tr, n_cols), lambda i: (i, 0))],
        out_specs=pl.BlockSpec((tr, n_cols), lambda i: (i, 0)),
        out_shape=jax.ShapeDtypeStruct((m, n_cols), out_dtype),
        compiler_params=_cparams(1),
        name="select",
    )(scores)


def _attn_prompt_kernel(q_ref, k_ref, v_ref, b_ref, o_ref, m_ref, l_ref, acc_ref, *, tq, ts):
    i, j = pl.program_id(1), pl.program_id(2)
    n_rep = ts // LANES

    @pl.when(j == 0)
    def _():
        m_ref[...] = jnp.full(m_ref.shape, NEG_BIG, F32)
        l_ref[...] = jnp.zeros(l_ref.shape, F32)
        acc_ref[...] = jnp.zeros(acc_ref.shape, F32)

    @pl.when(j * ts <= i * tq + tq - 1)
    def _():
        bias = b_ref[...].astype(F32)
        nt = (((1,), (1,)), ((), ()))
        heads = [slice(h * DH_A, (h + 1) * DH_A) for h in range(H_A)]
        scores = [lax.dot_general(q_ref[:, sl], k_ref[:, sl], nt, preferred_element_type=F32) + bias
                  for sl in heads]
        probs, alphas = [], []
        for h, s in enumerate(scores):
            m_prev = m_ref[h]
            m_next = jnp.maximum(m_prev, jnp.max(s, axis=1, keepdims=True))
            alphas.append(jnp.exp2(m_prev - m_next))
            probs.append(jnp.exp2(s - jnp.tile(m_next, (1, n_rep))).astype(BF16))
            m_ref[h] = m_next
        for h, (sl, p, alpha) in enumerate(zip(heads, probs, alphas)):
            pv = jnp.dot(p, v_ref[:, 2 * h * DH_A:(2 * h + 2) * DH_A], preferred_element_type=F32)
            acc_ref[:, sl] = acc_ref[:, sl] * alpha + pv[:, :DH_A]
            l_ref[h] = l_ref[h] * alpha + pv[:, DH_A:]

    @pl.when(j == pl.num_programs(2) - 1)
    def _():
        for h in range(H_A):
            sl = slice(h * DH_A, (h + 1) * DH_A)
            o_ref[:, sl] = (acc_ref[:, sl] / l_ref[h]).astype(o_ref.dtype)


def _attn_prompt(q_bf, k_bf, v_bf, bias, n_batch, t_len, tq=256, ts=512):
    nq, ns = t_len // tq, t_len // ts
    last = lambda i: (i * tq + tq - 1) // ts
    kern = functools.partial(_attn_prompt_kernel, tq=tq, ts=ts)
    return pl.pallas_call(
        kern,
        grid=(n_batch, nq, ns),
        in_specs=[pl.BlockSpec((tq, WA), lambda b, i, j: (b * nq + i, 0)),
                  pl.BlockSpec((ts, WA), lambda b, i, j: (b * ns + jnp.minimum(j, last(i)), 0)),
                  pl.BlockSpec((ts, 2 * WA), lambda b, i, j: (b * ns + jnp.minimum(j, last(i)), 0)),
                  pl.BlockSpec((tq, ts), lambda b, i, j: (b * nq + i, jnp.minimum(j, last(i))))],
        out_specs=pl.BlockSpec((tq, WA), lambda b, i, j: (b * nq + i, 0)),
        out_shape=jax.ShapeDtypeStruct((n_batch * t_len, WA), F32),
        scratch_shapes=[pltpu.VMEM((H_A, tq, LANES), F32), pltpu.VMEM((H_A, tq, LANES), F32),
                        pltpu.VMEM((tq, WA), F32)],
        compiler_params=_cparams(3),
        name="attn_prompt",
    )(q_bf, k_bf, v_bf, bias)


def _log_sigmoid(x):
    return jnp.minimum(x, 0.0) - jnp.log1p(jnp.exp(-jnp.abs(x)))


def _bf16_split3(x):
    x1 = x.astype(BF16)
    r1 = x - x1.astype(F32)
    x2 = r1.astype(BF16)
    x3 = (r1 - x2.astype(F32)).astype(BF16)
    return x1, x2, x3


def _gla_kernel(q_ref, k_ref, v_ref, tail_ref, wg_ref, bg_ref, s0_ref, o_ref, sfin_ref,
                st_ref, cum_all, kk_all, xs_all, *, tb, chunk, sub):
    c_id = pl.program_id(1)
    n_sub = chunk // sub
    nt = (((1,), (1,)), ((), ()))
    tn = (((0,), (0,)), ((), ()))

    @pl.when(c_id == 0)
    def _():
        for h in range(H_B):
            st_ref[h] = s0_ref[h].T

    rows = lax.broadcasted_iota(I32, (chunk, LANES), 0)
    cols = lax.broadcasted_iota(I32, (chunk, LANES), 1)
    tri = (lax.broadcasted_iota(I32, (chunk, chunk), 1)
           <= lax.broadcasted_iota(I32, (chunk, chunk), 0)).astype(BF16)
    ones = jnp.ones((LANES, LANES), BF16)

    def load(ref, r0, n_rows, c0, width):
        x = ref[pl.ds(r0, n_rows), c0:c0 + width]
        if n_rows < chunk:
            x = jnp.concatenate([x, jnp.zeros((chunk - n_rows, width), x.dtype)], axis=0)
        return x

    n_rows = min(tb, chunk)
    n_chunks = max(tb // chunk, 1)
    items = [(c, h) for c in range(n_chunks) for h in range(H_B)]
    slot_of = {it: i for i, it in enumerate(items)}
    lane_hit = [(cols % sub == sl) & (cols // sub == rows // sub) & (cols < chunk)
                for sl in range(sub)]
    row_blk = rows // sub

    ab = [load(tail_ref, c * chunk, n_rows, 0, N_TAIL)[:, L_AB:L_AB + GATE_RANK].astype(BF16)
          for c in range(n_chunks)]
    wg = [wg_ref[:, h * DK_B:(h + 1) * DK_B].astype(BF16) for h in range(H_B)]
    q, k, vb, la = {}, {}, {}, {}
    for c, h in items:
        q[c, h] = load(q_ref, c * chunk, n_rows, h * DK_B, DK_B) * (DK_B ** -0.5)
        k[c, h] = load(k_ref, c * chunk, n_rows, h * DK_B, DK_B)
        vb[c, h] = load(v_ref, c * chunk, n_rows, h * DV_B, DV_B).astype(BF16)
        x = jnp.dot(ab[c], wg[h], preferred_element_type=F32) + bg_ref[:, h * DK_B:(h + 1) * DK_B]
        la[c, h] = _log_sigmoid(x) / GATE_TAU
        if n_rows < chunk:
            la[c, h] = jnp.where(rows < n_rows, la[c, h], 0.0)

    cum = {}
    for it in items:
        l1, l2, l3 = _bf16_split3(la[it])
        cum[it] = (jnp.dot(tri, l1, preferred_element_type=F32)
                   + jnp.dot(tri, l2, preferred_element_type=F32)
                   + jnp.dot(tri, l3, preferred_element_type=F32))
        cum_all[slot_of[it]] = cum[it]
        kk_all[slot_of[it]] = k[it]

    att = {}
    for it in items:
        cum_ref = cum_all.at[slot_of[it]]
        a = jnp.zeros((chunk, LANES), F32)
        for i in range(1, n_sub):
            r_i = cum_ref[i * sub - 1:i * sub, :]
            qt = (q[it] * jnp.exp(jnp.minimum(cum[it] - r_i, 0.0))).astype(BF16)
            kt = (k[it] * jnp.exp(jnp.minimum(r_i - cum[it], 0.0))).astype(BF16)
            a_i = lax.dot_general(qt, kt, nt, preferred_element_type=F32)
            if chunk < LANES:
                a_i = jnp.concatenate([a_i, jnp.zeros((chunk, LANES - chunk), F32)], axis=1)
            a = jnp.where((row_blk == i) & (cols < i * sub), a_i, a)
        att[it] = a

    for it in items:
        cum_ref, kk_ref, xs_ref = (r.at[slot_of[it]] for r in (cum_all, kk_all, xs_all))
        for sl in range(sub):
            cum_s = jnp.concatenate(
                [jnp.broadcast_to(cum_ref[i * sub + sl:i * sub + sl + 1, :], (sub, LANES))
                 for i in range(n_sub)], axis=0)
            k_s = jnp.concatenate(
                [jnp.broadcast_to(kk_ref[i * sub + sl:i * sub + sl + 1, :], (sub, LANES))
                 for i in range(n_sub)], axis=0)
            dec = jnp.where(rows % sub >= sl, jnp.exp(jnp.minimum(cum[it] - cum_s, 0.0)), 0.0)
            xs_ref[sl * chunk:(sl + 1) * chunk, :] = (q[it] * k_s * dec).astype(BF16)
    n_x = sub * chunk
    red = jnp.dot(xs_all[...].reshape(len(items) * n_x, LANES), ones, preferred_element_type=F32)

    o_intra, upd, qdec, sdec = {}, {}, {}, {}
    for it in items:
        a = att[it]
        base = slot_of[it] * n_x
        for sl in range(sub):
            a = jnp.where(lane_hit[sl], red[base + sl * chunk:base + (sl + 1) * chunk, :], a)
        o_intra[it] = jnp.dot(a[:, :chunk].astype(BF16), vb[it], preferred_element_type=F32)
        last = cum_all[slot_of[it], chunk - 1:chunk, :]
        kd = (k[it] * jnp.exp(last - cum[it])).astype(BF16)
        upd[it] = lax.dot_general(vb[it], kd, tn, preferred_element_type=F32)
        qdec[it] = (q[it] * jnp.exp(cum[it])).astype(BF16)
        sdec[it] = jnp.exp(last)

    for h in range(H_B):
        st = st_ref[h]
        for c in range(n_chunks):
            o = o_intra[c, h] + lax.dot_general(qdec[c, h], st.astype(BF16), nt,
                                                preferred_element_type=F32)
            o_ref[c * chunk:c * chunk + n_rows, h * DV_B:(h + 1) * DV_B] = o[:n_rows]
            st = st * sdec[c, h] + upd[c, h]
        st_ref[h] = st

    @pl.when(c_id == pl.num_programs(1) - 1)
    def _():
        for h in range(H_B):
            sfin_ref[h] = st_ref[h].T


def _gla(z, tail, w_gate_up, b_gate, s0, n_batch, t_len, tb, chunk, sub):
    nb = t_len // tb
    n_slots = max(tb // chunk, 1) * H_B
    kern = functools.partial(_gla_kernel, tb=tb, chunk=chunk, sub=sub)
    return pl.pallas_call(
        kern,
        grid=(n_batch, nb),
        in_specs=[pl.BlockSpec((tb, WKB), lambda b, c: (b * nb + c, C_QB // WKB)),
                  pl.BlockSpec((tb, WKB), lambda b, c: (b * nb + c, C_KB // WKB)),
                  pl.BlockSpec((tb, WVB), lambda b, c: (b * nb + c, C_VB // WVB)),
                  pl.BlockSpec((tb, N_TAIL), lambda b, c: (b * nb + c, 0)),
                  pl.BlockSpec((GATE_RANK, WKB), lambda b, c: (0, 0)),
                  pl.BlockSpec((1, WKB), lambda b, c: (0, 0)),
                  pl.BlockSpec((None, H_B, DK_B, DV_B), lambda b, c: (b, 0, 0, 0))],
        out_specs=[pl.BlockSpec((tb, WVB), lambda b, c: (b * nb + c, 0)),
                   pl.BlockSpec((None, H_B, DK_B, DV_B), lambda b, c: (b, 0, 0, 0))],
        out_shape=[jax.ShapeDtypeStruct((n_batch * t_len, WVB), F32),
                   jax.ShapeDtypeStruct((n_batch, H_B, DK_B, DV_B), F32)],
        scratch_shapes=[pltpu.VMEM((H_B, DV_B, DK_B), F32), pltpu.VMEM((n_slots, chunk, LANES), F32),
                        pltpu.VMEM((n_slots, chunk, LANES), F32),
                        pltpu.VMEM((n_slots, sub * chunk, LANES), BF16)],
        compiler_params=_cparams(2),
        name="gla",
    )(z, z, z, tail, w_gate_up, b_gate, s0)


def _silu(x):
    return x / (1.0 + jnp.exp(-x))


def _merge_kernel(att_ref, ga_ref, ob_ref, gb_ref, gn_ref, wo_ref, x_ref, nf_ref, o_ref):
    a = (att_ref[...] * _silu(ga_ref[...])).astype(BF16)
    y = jnp.dot(a, wo_ref[:WA, :], preferred_element_type=F32)
    gn = gn_ref[...]
    for h in range(H_B):
        sl = slice(h * DV_B, (h + 1) * DV_B)
        ob = ob_ref[:, sl]
        bn = ob * lax.rsqrt(jnp.mean(ob * ob, axis=-1, keepdims=True) + EPS) * gn
        bp = (bn * _silu(gb_ref[:, sl])).astype(BF16)
        y = y + jnp.dot(bp, wo_ref[WA + h * DV_B:WA + (h + 1) * DV_B, :], preferred_element_type=F32)
    xo = x_ref[...] + y
    o_ref[...] = xo * lax.rsqrt(jnp.mean(xo * xo, axis=-1, keepdims=True) + EPS) * nf_ref[...]


def _merge(att, z, ob, gla_norm, w_out_bf, x2d, norm_f, tm):
    m = x2d.shape[0]
    return pl.pallas_call(
        _merge_kernel,
        grid=(m // tm,),
        in_specs=[pl.BlockSpec((tm, WA), lambda i: (i, 0)),
                  pl.BlockSpec((tm, WA), lambda i: (i, C_GA // WA)),
                  pl.BlockSpec((tm, WVB), lambda i: (i, 0)),
                  pl.BlockSpec((tm, WVB), lambda i: (i, C_GB // WVB)),
                  pl.BlockSpec((1, DV_B), lambda i: (0, 0)),
                  pl.BlockSpec((WA + WVB, D_MODEL), lambda i: (0, 0)),
                  pl.BlockSpec((tm, D_MODEL), lambda i: (i, 0)),
                  pl.BlockSpec((1, D_MODEL), lambda i: (0, 0))],
        out_specs=pl.BlockSpec((tm, D_MODEL), lambda i: (i, 0)),
        out_shape=jax.ShapeDtypeStruct((m, D_MODEL), F32),
        compiler_params=_cparams(1),
        name="merge",
    )(att, z, ob, z, gla_norm, w_out_bf, x2d, norm_f)


PAGES_PER_STEP = 16


def _idx_pages_kernel(pt_ref, q_ref, w_ref, *refs, n_pages, t_dec, causal):
    page_refs, o_ref = refs[:n_pages], refs[n_pages]
    q = q_ref[...]
    w = w_ref[...]
    for p in range(n_pages):
        kp = page_refs[p][...].astype(BF16)
        d = jnp.dot(q, kp, preferred_element_type=F32)
        r = w * jnp.maximum(d, 0.0)
        acc = r[0:t_dec]
        for h in range(1, H_IDX):
            acc = acc + r[h * t_dec:(h + 1) * t_dec]
        if causal:
            t_i = lax.broadcasted_iota(I32, acc.shape, 0)
            s_i = lax.broadcasted_iota(I32, acc.shape, 1)
            acc = jnp.where(s_i <= t_i, acc, -jnp.inf)
        o_ref[:, p * PAGE_SIZE:(p + 1) * PAGE_SIZE] = acc


def _idx_sample_past(page_table, q_stk, w_stk, pool_ik, n_seq, t_dec, n_pages):
    pps = PAGES_PER_STEP
    n_g = n_pages // pps
    rows = H_IDX * t_dec
    kern = functools.partial(_idx_pages_kernel, n_pages=pps, t_dec=t_dec, causal=False)
    page_spec = lambda r: pl.BlockSpec(
        (None, D_IDX, PAGE_SIZE), lambda b, g, pt, r=r: (pt[b * n_pages + g * pps + r], 0, 0))
    return pl.pallas_call(
        kern,
        grid_spec=pltpu.PrefetchScalarGridSpec(
            num_scalar_prefetch=1,
            grid=(n_seq, n_g),
            in_specs=[pl.BlockSpec((rows, D_IDX), lambda b, g, pt: (b, 0)),
                      pl.BlockSpec((rows, 1), lambda b, g, pt: (b, 0))]
                     + [page_spec(r) for r in range(pps)],
            out_specs=pl.BlockSpec((t_dec, pps * PAGE_SIZE), lambda b, g, pt: (b, g))),
        out_shape=jax.ShapeDtypeStruct((n_seq * t_dec, n_pages * PAGE_SIZE), F32),
        compiler_params=_cparams(2),
        name="idx_sample_past",
    )(page_table.reshape(-1), q_stk, w_stk, *([pool_ik] * pps))


def _idx_sample_new(q_stk, w_stk, ki_new_pad, n_seq, t_dec):
    rows = H_IDX * t_dec
    kern = functools.partial(_idx_pages_kernel, None, n_pages=1, t_dec=t_dec, causal=True)
    return pl.pallas_call(
        kern,
        grid=(n_seq,),
        in_specs=[pl.BlockSpec((rows, D_IDX), lambda b: (b, 0)),
                  pl.BlockSpec((rows, 1), lambda b: (b, 0)),
                  pl.BlockSpec((None, D_IDX, PAGE_SIZE), lambda b: (b, 0, 0))],
        out_specs=pl.BlockSpec((t_dec, PAGE_SIZE), lambda b: (b, 0)),
        out_shape=jax.ShapeDtypeStruct((n_seq * t_dec, PAGE_SIZE), F32),
        compiler_params=_cparams(1),
        name="idx_sample_new",
    )(q_stk, w_stk, ki_new_pad)


ATTN_PAGES_PER_STEP = 8


def _attn_sample_kernel(pt_ref, q_ref, sel_ref, selnew_ref, e_ref, knew_ref, vnew_ref, *refs,
                        n_pages, t_dec):
    k_refs, v_refs = refs[:n_pages], refs[n_pages:2 * n_pages]
    o_ref, m_ref, l_ref, acc_ref = refs[2 * n_pages:]
    g = pl.program_id(1)
    rows = H_A * t_dec
    nt = (((1,), (1,)), ((), ()))
    q = q_ref[...]
    n_col = PAGE_SIZE * H_A
    head_hot = (lax.broadcasted_iota(I32, (rows, LANES), 0) // t_dec
                == lax.broadcasted_iota(I32, (rows, LANES), 1)).astype(BF16)

    def update(sels, k_pages, v_pages):
        lhs = jnp.concatenate(
            [jnp.concatenate([jnp.tile(sel, (H_A, 1)).astype(BF16), head_hot], axis=1) for sel in sels],
            axis=0)
        bias = jnp.dot(lhs, e_ref[...], preferred_element_type=F32)
        scores = []
        m_next = m_ref[...]
        for i, k_page in enumerate(k_pages):
            s = lax.dot_general(q, k_page[...].astype(BF16), nt, preferred_element_type=F32)
            s = s + bias[i * rows:(i + 1) * rows]
            m_next = jnp.maximum(m_next, jnp.max(s, axis=1, keepdims=True))
            scores.append(s)
        alpha = jnp.exp2(m_ref[...] - m_next)
        m_rep = jnp.tile(m_next, (1, n_col // LANES))
        l_new = alpha * l_ref[...]
        acc = acc_ref[...] * alpha
        for s, v_page in zip(scores, v_pages):
            p = jnp.exp2(s - m_rep)
            l_new = l_new + jnp.sum(p, axis=1, keepdims=True)
            acc = acc + jnp.dot(p.astype(BF16), v_page[...].astype(BF16), preferred_element_type=F32)
        m_ref[...] = m_next
        l_ref[...] = l_new
        acc_ref[...] = acc

    @pl.when(g == 0)
    def _():
        m_ref[...] = jnp.full(m_ref.shape, NEG_BIG, F32)
        l_ref[...] = jnp.zeros(l_ref.shape, F32)
        acc_ref[...] = jnp.zeros(acc_ref.shape, F32)
        update([selnew_ref[...]], [knew_ref], [vnew_ref])

    update([sel_ref[:, p_i * PAGE_SIZE:(p_i + 1) * PAGE_SIZE] for p_i in range(n_pages)],
           k_refs, v_refs)

    @pl.when(g == pl.num_programs(1) - 1)
    def _():
        o_ref[...] = acc_ref[...] / l_ref[...]


def _attn_sample(page_table, q_stk, sel_past, sel_new, expand, knew_pad, vnew_pad, pool_k, pool_v,
                 n_seq, t_dec, n_pages):
    pps = ATTN_PAGES_PER_STEP
    n_g = n_pages // pps
    rows = H_A * t_dec
    n_col = PAGE_SIZE * H_A
    kern = functools.partial(_attn_sample_kernel, n_pages=pps, t_dec=t_dec)
    page_spec = lambda r: pl.BlockSpec(
        (None, n_col, DH_A), lambda b, g, pt, r=r: (pt[b * n_pages + g * pps + r], 0, 0))
    return pl.pallas_call(
        kern,
        grid_spec=pltpu.PrefetchScalarGridSpec(
            num_scalar_prefetch=1,
            grid=(n_seq, n_g),
            in_specs=[pl.BlockSpec((rows, DH_A), lambda b, g, pt: (b, 0)),
                      pl.BlockSpec((t_dec, pps * PAGE_SIZE), lambda b, g, pt: (b, g)),
                      pl.BlockSpec((t_dec, PAGE_SIZE), lambda b, g, pt: (b, 0)),
                      pl.BlockSpec((2 * PAGE_SIZE, n_col), lambda b, g, pt: (0, 0)),
                      pl.BlockSpec((None, n_col, DH_A), lambda b, g, pt: (b, 0, 0)),
                      pl.BlockSpec((None, n_col, DH_A), lambda b, g, pt: (b, 0, 0))]
                     + [page_spec(r) for r in range(pps)] * 2,
            out_specs=pl.BlockSpec((rows, DH_A), lambda b, g, pt: (b, 0)),
            scratch_shapes=[pltpu.VMEM((rows, LANES), F32), pltpu.VMEM((rows, LANES), F32),
                            pltpu.VMEM((rows, DH_A), F32)]),
        out_shape=jax.ShapeDtypeStruct((n_seq * rows, DH_A), F32),
        compiler_params=_cparams(2),
        name="attn_sample",
    )(page_table.reshape(-1), q_stk, sel_past, sel_new, expand, knew_pad, vnew_pad,
      *([pool_k] * pps), *([pool_v] * pps))


def _group_common(x, norm_in, w_parts, pos, tm, tn, tr):
    n_seq, t_len, _ = x.shape
    x2d = x.reshape(n_seq * t_len, D_MODEL)
    w_full, w_gla, w_tail = w_parts
    z = _proj(x2d, norm_in, w_full, C_QB, w_gla, tm, tn)
    tail = _proj(x2d, norm_in, w_tail, N_TAIL, None, tm, N_TAIL)
    tab = _rope_tables(pos)
    return x2d, z, tail, _rope(z, tail, tab, tr)


def kernel(x_prompt, x_sample, cache_k, cache_v, cache_idx_k, state_gla, page_table,
           norm_in, w_in, w_gate_up, b_gate, gla_norm, w_out, norm_f):
    n_p, t_p, _ = x_prompt.shape
    n_s, t_s, _ = x_sample.shape
    n_pages = page_table.shape[1]
    past = n_pages * PAGE_SIZE
    assert cache_k.shape[0] == 1, "single layer"

    w_parts = _prep_w_in(w_in[0])
    w_out_bf = w_out[0].astype(BF16)
    g_in, g_gla, g_f = norm_in[0][None], gla_norm[0][None], norm_f[None]
    wg, bg = w_gate_up[0], b_gate[0][None]

    xp2d, zp, tailp, (newk_p, newv_p, newik_p, q_p, k_p, v_p, qi_p, kie_p, kio_p) = _group_common(
        x_prompt, g_in, w_parts, jnp.arange(t_p), 1024, 512, 256)
    sc_p = _idx_prompt(qi_p, tailp, kie_p, kio_p, n_p, t_p)
    n_groups_p = max(1, min(4, t_p // 1024))
    bias_p = _select(sc_p, min(TOPK_MAX, t_p // 4), 4, t_p, BF16, n_half=2, n_groups=n_groups_p)
    att_p = _attn_prompt(q_p, k_p, v_p, bias_p, n_p, t_p)
    s0_p = jnp.zeros((n_p, H_B, DK_B, DV_B), F32)
    ob_p, sfin_p = _gla(zp, tailp, wg, bg, s0_p, n_p, t_p, 256, 64, 16)
    y_p = _merge(att_p, zp, ob_p, g_gla, w_out_bf, xp2d, g_f, 256)

    m_s = n_s * t_s
    pos_s = jnp.tile(past + jnp.arange(t_s), n_s)
    xs2d, zs, tails, (newk_s, newv_s, newik_s, q_s, k_s, v_s, qi_s, kie_s, kio_s) = _group_common(
        x_sample, g_in, w_parts, pos_s, m_s, 1024, m_s)
    qi_stk = qi_s.reshape(n_s, t_s, H_IDX, D_IDX).transpose(0, 2, 1, 3).reshape(n_s * H_IDX * t_s, D_IDX)
    w_stk = (tails[:, L_WI:L_WI + H_IDX] * IDX_W_SCALE).reshape(n_s, t_s, H_IDX).transpose(0, 2, 1)
    w_stk = w_stk.reshape(n_s * H_IDX * t_s, 1)
    q_stk = q_s.reshape(n_s, t_s, H_A, DH_A).transpose(0, 2, 1, 3).reshape(n_s * H_A * t_s, DH_A)
    pad_rows = lambda a, n: jnp.pad(a, ((0, 0), (0, n - a.shape[1]), (0, 0)))
    ki_new_pad = jnp.pad(newik_s.reshape(n_s, t_s, D_IDX).transpose(0, 2, 1),
                         ((0, 0), (0, 0), (0, PAGE_SIZE - t_s)))
    knew_pad = pad_rows(newk_s.reshape(n_s, t_s * H_A, DH_A), PAGE_SIZE * H_A)
    vnew_pad = pad_rows(newv_s.reshape(n_s, t_s * H_A, DH_A), PAGE_SIZE * H_A)
    pool_ik = jnp.swapaxes(cache_idx_k[0], 1, 2)
    pool_k = cache_k[0].reshape(-1, PAGE_SIZE * H_A, DH_A)
    pool_v = cache_v[0].reshape(-1, PAGE_SIZE * H_A, DH_A)

    sc_past = _idx_sample_past(page_table, qi_stk, w_stk, pool_ik, n_s, t_s, n_pages)
    sc_new = _idx_sample_new(qi_stk, w_stk, ki_new_pad, n_s, t_s)
    sc_s = jnp.concatenate([sc_past, sc_new], axis=1)
    k_top_s = min(TOPK_MAX, (past + t_s) // 4)
    n_tiles_s = sc_s.shape[1] // LANES
    sub_s = max(d for d in range(1, 6) if n_tiles_s % d == 0)
    bias_s = _select(sc_s, k_top_s, sub_s, 0, F32)
    col = jnp.arange(PAGE_SIZE * H_A)[None, :]
    row = jnp.arange(2 * PAGE_SIZE)[:, None]
    expand = jnp.where(row < PAGE_SIZE, (col // H_A == row).astype(F32),
                       jnp.where((row < PAGE_SIZE + H_A) & (col % H_A != row - PAGE_SIZE), NEG_BIG, 0.0)
                       ).astype(BF16)
    att_stk = _attn_sample(page_table, q_stk, bias_s[:, :past], bias_s[:, past:], expand,
                           knew_pad, vnew_pad, pool_k, pool_v, n_s, t_s, n_pages)
    att_s = att_stk.reshape(n_s, H_A, t_s, DH_A).transpose(0, 2, 1, 3).reshape(m_s, WA)
    ob_s, sfin_s = _gla(zs, tails, wg, bg, state_gla[0], n_s, t_s, t_s, 16, 16)
    y_s = _merge(att_s, zs, ob_s, g_gla, w_out_bf, xs2d, g_f, m_s)

    return (y_p.reshape(n_p, t_p, D_MODEL), y_s.reshape(n_s, t_s, D_MODEL),
            newk_p.reshape(1, n_p, t_p, H_A, DH_A), newv_p.reshape(1, n_p, t_p, H_A, DH_A),
            newik_p.reshape(1, n_p, t_p, D_IDX), sfin_p[None],
            newk_s.reshape(1, n_s, t_s, H_A, DH_A), newv_s.reshape(1, n_s, t_s, H_A, DH_A),
            newik_s.reshape(1, n_s, t_s, D_IDX), sfin_s[None])
```

```python
import functools
import math

import jax
import jax.numpy as jnp
import numpy as np
from jax import lax
from jax.experimental import pallas as pl
from jax.experimental.pallas import tpu as pltpu

F32 = jnp.float32
BF16 = jnp.bfloat16
I32 = jnp.int32

D_MODEL = 2048
PAGE_SIZE = 128
DH_A = 128
H_A = 8
H_IDX = 16
D_IDX = 64
TOPK_MAX = 256
DK_B = 128
DV_B = 256
H_B = 4
GATE_RANK = 16
GATE_TAU = 16.0
ROPE_THETA = 10000.0
EPS = 1e-6
NEG_BIG = -1e30
WA = H_A * DH_A
WI = H_IDX * D_IDX
WKB = H_B * DK_B
WVB = H_B * DV_B

C_QA, C_KA, C_VA, C_GA, C_QI = 0, WA, 2 * WA, 3 * WA, 4 * WA
C_QB = C_QI + WI
C_KB = C_QB + WKB
C_VB = C_KB + WKB
C_GB = C_VB + WVB
N_MAIN = C_GB + WVB
L_KI, L_WI, L_AB = 0, D_IDX, D_IDX + H_IDX
N_TAIL = 128

LANES = 128
INT_MIN = -2 ** 31
VMEM_LIMIT = 48 * 1024 * 1024
FUSED_VMEM_LIMIT = 56 * 1024 * 1024


def _prep_w_in(w_in):
    o_ki = C_QI + WI
    o_qb = o_ki + D_IDX + H_IDX
    o_ab = o_qb + 2 * WKB + 2 * WVB
    w_bf = w_in.astype(BF16)
    pad = jnp.zeros((w_in.shape[0], N_TAIL - D_IDX - H_IDX - GATE_RANK), BF16)
    tail = jnp.concatenate([w_bf[:, o_ki:o_qb], w_bf[:, o_ab:], pad], axis=1)
    return w_bf, w_bf[:, o_qb:o_ab], tail


def _cparams(n_axes):
    return pltpu.CompilerParams(dimension_semantics=("arbitrary",) * n_axes,
                                vmem_limit_bytes=VMEM_LIMIT)


def _proj_kernel(x_ref, g_ref, wa_ref, wb_ref, o_ref, h_ref, *, n_a):
    _proj_body(pl.program_id(1), x_ref, g_ref, wa_ref, wb_ref, o_ref, h_ref, n_a)


def _proj_body(j, x_ref, g_ref, wa_ref, wb_ref, o_ref, h_ref, n_a):
    @pl.when(j == 0)
    def _():
        x = x_ref[...]
        ms = jnp.mean(x * x, axis=-1, keepdims=True)
        h_ref[...] = (x * lax.rsqrt(ms + EPS) * g_ref[...]).astype(BF16)

    @pl.when(j < n_a)
    def _():
        o_ref[...] = jnp.dot(h_ref[...], wa_ref[...], preferred_element_type=F32)

    @pl.when(j >= n_a)
    def _():
        o_ref[...] = jnp.dot(h_ref[...], wb_ref[...], preferred_element_type=F32)


def _proj(x2d, g, w_a, n_cols_a, w_b, tm, tn):
    m, k = x2d.shape
    n_a = n_cols_a // tn
    if w_b is None:
        w_b, n_b = w_a, 0
    else:
        n_b = w_b.shape[1] // tn
    return pl.pallas_call(
        functools.partial(_proj_kernel, n_a=n_a),
        grid=(m // tm, n_a + n_b),
        in_specs=[pl.BlockSpec((tm, k), lambda i, j: (i, 0)),
                  pl.BlockSpec((1, k), lambda i, j: (0, 0)),
                  pl.BlockSpec((k, tn), lambda i, j: (0, jnp.minimum(j, n_a - 1))),
                  pl.BlockSpec((k, tn), lambda i, j: (0, jnp.maximum(j - n_a, 0)))],
        out_specs=pl.BlockSpec((tm, tn), lambda i, j: (i, j)),
        out_shape=jax.ShapeDtypeStruct((m, (n_a + n_b) * tn), F32),
        scratch_shapes=[pltpu.VMEM((tm, k), BF16)],
        compiler_params=_cparams(2),
        name="proj",
    )(x2d, g, w_a, w_b)


def _rope128(x, cos, sin_signed):
    return x * cos + pltpu.roll(x, 64, axis=1) * sin_signed


def _rope64(x, cos, sin_lo, sin_hi):
    return x * cos + pltpu.roll(x, 96, axis=1) * sin_lo + pltpu.roll(x, 32, axis=1) * sin_hi


def _rope_kernel(qa_ref, ka_ref, va_ref, qi_ref, tail_ref, tab_ref,
                 newk_ref, newv_ref, newik_ref, qbf_ref, kbf_ref, vbf_ref, qibf_ref,
                 kie_ref, kio_ref):
    cos_a, sin_a = tab_ref[0], tab_ref[1]
    cos_i, sin_lo, sin_hi = tab_ref[2], tab_ref[3], tab_ref[4]
    scale = (DH_A ** -0.5) * math.log2(math.e)
    for h in range(H_A):
        sl = slice(h * DH_A, (h + 1) * DH_A)
        q = _rope128(qa_ref[:, sl], cos_a, sin_a)
        qbf_ref[:, sl] = (q * scale).astype(BF16)
        k = _rope128(ka_ref[:, sl], cos_a, sin_a)
        newk_ref[:, sl] = k
        kbf_ref[:, sl] = k.astype(BF16)
    v = va_ref[...]
    newv_ref[...] = v
    ones = jnp.ones((v.shape[0], DH_A), BF16)
    for h in range(H_A):
        vbf_ref[:, 2 * h * DH_A:(2 * h + 1) * DH_A] = v[:, h * DH_A:(h + 1) * DH_A].astype(BF16)
        vbf_ref[:, (2 * h + 1) * DH_A:(2 * h + 2) * DH_A] = ones
    for j in range(WI // LANES):
        sl = slice(j * LANES, (j + 1) * LANES)
        qibf_ref[:, sl] = _rope64(qi_ref[:, sl], cos_i, sin_lo, sin_hi).astype(BF16)
    ki = _rope64(tail_ref[...], cos_i, sin_lo, sin_hi)
    newik_ref[...] = ki[:, L_KI:L_KI + D_IDX]
    lane = lax.broadcasted_iota(I32, ki.shape, 1)
    ki_lo = jnp.where(lane < D_IDX, ki, 0.0)
    kie_ref[...] = ki_lo.astype(BF16)
    kio_ref[...] = pltpu.roll(ki_lo, D_IDX, axis=1).astype(BF16)


def _rope(z, tail, tab, tm):
    m = z.shape[0]
    n_tab = tab.shape[1] // tm
    col = lambda c: pl.BlockSpec((tm, WA), lambda i, c=c: (i, c))
    row = lambda w: pl.BlockSpec((tm, w), lambda i: (i, 0))
    return pl.pallas_call(
        _rope_kernel,
        grid=(m // tm,),
        in_specs=[col(C_QA // WA), col(C_KA // WA), col(C_VA // WA), col(C_QI // WA),
                  row(N_TAIL),
                  pl.BlockSpec((5, tm, LANES), lambda i: (0, i % n_tab, 0))],
        out_specs=[row(WA), row(WA), row(D_IDX), row(WA), row(WA), row(2 * WA), row(WI),
                   row(LANES), row(LANES)],
        out_shape=[jax.ShapeDtypeStruct((m, WA), F32), jax.ShapeDtypeStruct((m, WA), F32),
                   jax.ShapeDtypeStruct((m, D_IDX), F32),
                   jax.ShapeDtypeStruct((m, WA), BF16), jax.ShapeDtypeStruct((m, WA), BF16),
                   jax.ShapeDtypeStruct((m, 2 * WA), BF16), jax.ShapeDtypeStruct((m, WI), BF16),
                   jax.ShapeDtypeStruct((m, LANES), BF16), jax.ShapeDtypeStruct((m, LANES), BF16)],
        compiler_params=_cparams(1),
        name="rope",
    )(z, z, z, z, tail, tab)


def _rope_tables(pos):
    pos = pos.astype(F32)[:, None]
    half_a, half_i = DH_A // 2, D_IDX // 2
    inv_a = ROPE_THETA ** (-jnp.arange(half_a, dtype=F32) / half_a)
    inv_i = ROPE_THETA ** (-jnp.arange(half_i, dtype=F32) / half_i)
    ca, sa = jnp.cos(pos * inv_a), jnp.sin(pos * inv_a)
    ci, si = jnp.cos(pos * inv_i), jnp.sin(pos * inv_i)
    zi = jnp.zeros_like(si)
    return jnp.stack([
        jnp.concatenate([ca, ca], axis=1),
        jnp.concatenate([-sa, sa], axis=1),
        jnp.concatenate([ci, ci, ci, ci], axis=1),
        jnp.concatenate([-si, zi, -si, zi], axis=1),
        jnp.concatenate([zi, si, zi, si], axis=1)])


IDX_W_SCALE = (H_IDX ** -0.5) * (D_IDX ** -0.5)


def _idx_scores(q_ref, w, kie, kio):
    nt = (((1,), (1,)), ((), ()))
    acc = jnp.zeros((q_ref.shape[0], kie.shape[0]), F32)
    for j in range(H_IDX // 2):
        qp = q_ref[:, j * LANES:(j + 1) * LANES]
        de = lax.dot_general(qp, kie, nt, preferred_element_type=F32)
        acc = acc + w[:, 2 * j:2 * j + 1] * jnp.maximum(de, 0.0)
        do = lax.dot_general(qp, kio, nt, preferred_element_type=F32)
        acc = acc + w[:, 2 * j + 1:2 * j + 2] * jnp.maximum(do, 0.0)
    return acc


def _idx_prompt_kernel(q_ref, tail_ref, kie_ref, kio_ref, o_ref, *, tq, tc, t_len):
    i = pl.program_id(1)
    w = tail_ref[:, L_WI:L_WI + H_IDX] * IDX_W_SCALE
    n_c = (i * tq + tq - 1) // tc + 1
    row = i * tq + lax.broadcasted_iota(I32, (tq, tc), 0)

    def body(c, carry):
        c0 = pl.multiple_of(c * tc, tc)
        acc = _idx_scores(q_ref, w, kie_ref[pl.ds(c0, tc), :], kio_ref[pl.ds(c0, tc), :])
        col = c0 + lax.broadcasted_iota(I32, (tq, tc), 1)
        o_ref[:, pl.ds(c0, tc)] = jnp.where(col <= row, acc, -jnp.inf)
        return carry

    lax.fori_loop(0, n_c, body, 0)

    def fill(c, carry):
        c0 = pl.multiple_of(c * tc, tc)
        o_ref[:, pl.ds(c0, tc)] = jnp.full((tq, tc), -jnp.inf, F32)
        return carry

    lax.fori_loop(n_c, t_len // tc, fill, 0)


def _idx_prompt(qi_bf, tail, kie, kio, n_batch, t_len, tq=128, tc=256):
    nq = t_len // tq
    kern = functools.partial(_idx_prompt_kernel, tq=tq, tc=tc, t_len=t_len)
    return pl.pallas_call(
        kern,
        grid=(n_batch, nq),
        in_specs=[pl.BlockSpec((tq, WI), lambda b, i: (b * nq + i, 0)),
                  pl.BlockSpec((tq, N_TAIL), lambda b, i: (b * nq + i, 0)),
                  pl.BlockSpec((t_len, LANES), lambda b, i: (b, 0)),
                  pl.BlockSpec((t_len, LANES), lambda b, i: (b, 0))],
        out_specs=pl.BlockSpec((tq, t_len), lambda b, i: (b * nq + i, 0)),
        out_shape=jax.ShapeDtypeStruct((n_batch * t_len, t_len), F32),
        compiler_params=_cparams(2),
        name="idx_prompt",
    )(qi_bf, tail, kie, kio)


SELECT_STEPS_PER_CHECK = 3
SELECT_MAX_CHECKS = 16
SELECT_ROWS = 128


def _select_body(s_ref, o_ref, *, n_half, sub, n_all, n_c, unroll, k_top):
    tc = sub * LANES
    kf = float(k_top)
    halves = [pl.ds(hh * SELECT_ROWS, SELECT_ROWS) for hh in range(n_half)]
    shape = (SELECT_ROWS, LANES)

    def over_chunks(fn, init, lo_c, hi_c):
        if unroll:
            carry = init
            for c in range(lo_c, hi_c):
                carry = fn(c * tc, carry)
            return carry
        return lax.fori_loop(lo_c, hi_c, lambda c, carry: fn(pl.multiple_of(c * tc, tc), carry), init)

    def stats(c0, carry):
        out = []
        for rows, (mn, mx, cnt) in zip(halves, carry):
            for u in range(sub):
                x = s_ref[rows, pl.ds(c0 + u * LANES, LANES)]
                valid = x > -jnp.inf
                mn = jnp.minimum(mn, jnp.where(valid, x, jnp.inf))
                mx = jnp.maximum(mx, x)
                cnt = cnt + jnp.where(valid, 1.0, 0.0)
            out.append((mn, mx, cnt))
        return tuple(out)

    st = over_chunks(stats, tuple((jnp.full(shape, jnp.inf, F32), jnp.full(shape, -jnp.inf, F32),
                                   jnp.zeros(shape, F32)) for _ in halves), 0, n_c)
    lo0 = tuple(jnp.min(mn, axis=1, keepdims=True) for mn, _, _ in st)
    hi0 = tuple(jnp.max(mx, axis=1, keepdims=True) for _, mx, _ in st)
    c0_lo = tuple(jnp.sum(cnt, axis=1, keepdims=True) for _, _, cnt in st)

    def count_ge(thrs):
        thr_b = [jnp.broadcast_to(t, shape) for t in thrs]

        def body(c0, accs):
            out = []
            for rows, tb, acc in zip(halves, thr_b, accs):
                for u in range(sub):
                    x = s_ref[rows, pl.ds(c0 + u * LANES, LANES)]
                    acc = acc + jnp.where(x >= tb, 1.0, 0.0)
                out.append(acc)
            return tuple(out)

        accs = over_chunks(body, tuple(jnp.zeros(shape, F32) for _ in halves), 0, n_c)
        return [jnp.sum(a, axis=1, keepdims=True) for a in accs]

    def unresolved(state):
        it, _, _, c_lo = state
        worst = c_lo[0]
        for c in c_lo[1:]:
            worst = jnp.maximum(worst, c)
        return jnp.logical_and(it < SELECT_MAX_CHECKS, jnp.max(worst) > kf)

    def refine(state):
        it, lo, hi, c_lo = state
        lo, hi, c_lo = list(lo), list(hi), list(c_lo)
        for _ in range(SELECT_STEPS_PER_CHECK):
            mid = [0.5 * a + 0.5 * b for a, b in zip(lo, hi)]
            cnt = count_ge(mid)
            for i in range(n_half):
                ok = cnt[i] >= kf
                lo[i] = jnp.where(ok, mid[i], lo[i])
                c_lo[i] = jnp.where(ok, cnt[i], c_lo[i])
                hi[i] = jnp.where(ok, hi[i], mid[i])
        return it + 1, tuple(lo), tuple(hi), tuple(c_lo)

    _, thr, _, _ = lax.while_loop(unresolved, refine, (jnp.int32(0), lo0, hi0, c0_lo))

    def emit(c0, carry):
        for rows, t in zip(halves, thr):
            x = s_ref[rows, pl.ds(c0, tc)]
            o_ref[rows, pl.ds(c0, tc)] = jnp.where(x >= t, 0.0, NEG_BIG).astype(o_ref.dtype)
        return carry

    over_chunks(emit, 0, 0, n_c)

    def fill(c0, carry):
        o_ref[:, pl.ds(c0, tc)] = jnp.full((n_half * SELECT_ROWS, tc), NEG_BIG, o_ref.dtype)
        return carry

    over_chunks(fill, 0, n_c, n_all)


def _select_kernel(s_ref, o_ref, *, n_half, sub, n_cols, k_top, rows_per_batch, n_groups):
    tc = sub * LANES
    n_all = n_cols // tc
    body = functools.partial(_select_body, s_ref, o_ref, n_half=n_half, sub=sub, n_all=n_all, k_top=k_top)
    if rows_per_batch:
        tr = n_half * SELECT_ROWS
        r0 = (pl.program_id(0) % (rows_per_batch // tr)) * tr
        rows_per_group = rows_per_batch // n_groups
        for g in range(n_groups):
            @pl.when(r0 // rows_per_group == g)
            def _(g=g):
                body(n_c=((g + 1) * rows_per_group - 1) // tc + 1, unroll=True)
    else:
        body(n_c=n_all, unroll=False)


def _select(scores, k_top, sub, rows_per_batch, out_dtype, n_half=1, n_groups=1):
    m, n_cols = scores.shape
    tr = n_half * SELECT_ROWS
    kern = functools.partial(_select_kernel, n_half=n_half, sub=sub, n_cols=n_cols, k_top=k_top,
                             rows_per_batch=rows_per_batch, n_groups=n_groups)
    return pl.pallas_call(
        kern,
        grid=(m // tr,),
        in_specs=[pl.BlockSpec((tr, n_cols), lambda i: (i, 0))],
        out_specs=pl.BlockSpec((tr, n_cols), lambda i: (i, 0)),
        out_shape=jax.ShapeDtypeStruct((m, n_cols), out_dtype),
        compiler_params=_cparams(1),
        name="select",
    )(scores)


def _attn_prompt_kernel(q_ref, k_ref, v_ref, b_ref, o_ref, m_ref, l_ref, acc_ref, *, tq, ts):
    i, j = pl.program_id(1), pl.program_id(2)
    n_rep = ts // LANES

    @pl.when(j == 0)
    def _():
        m_ref[...] = jnp.full(m_ref.shape, NEG_BIG, F32)
        l_ref[...] = jnp.zeros(l_ref.shape, F32)
        acc_ref[...] = jnp.zeros(acc_ref.shape, F32)

    @pl.when(j * ts <= i * tq + tq - 1)
    def _():
        bias = b_ref[...].astype(F32)
        nt = (((1,), (1,)), ((), ()))
        heads = [slice(h * DH_A, (h + 1) * DH_A) for h in range(H_A)]
        scores = [lax.dot_general(q_ref[:, sl], k_ref[:, sl], nt, preferred_element_type=F32) + bias
                  for sl in heads]
        probs, alphas = [], []
        for h, s in enumerate(scores):
            m_prev = m_ref[h]
            m_next = jnp.maximum(m_prev, jnp.max(s, axis=1, keepdims=True))
            alphas.append(jnp.exp2(m_prev - m_next))
            probs.append(jnp.exp2(s - jnp.tile(m_next, (1, n_rep))).astype(BF16))
            m_ref[h] = m_next
        for h, (sl, p, alpha) in enumerate(zip(heads, probs, alphas)):
            pv = jnp.dot(p, v_ref[:, 2 * h * DH_A:(2 * h + 2) * DH_A], preferred_element_type=F32)
            acc_ref[:, sl] = acc_ref[:, sl] * alpha + pv[:, :DH_A]
            l_ref[h] = l_ref[h] * alpha + pv[:, DH_A:]

    @pl.when(j == pl.num_programs(2) - 1)
    def _():
        for h in range(H_A):
            sl = slice(h * DH_A, (h + 1) * DH_A)
            o_ref[:, sl] = (acc_ref[:, sl] / l_ref[h]).astype(o_ref.dtype)


def _attn_prompt(q_bf, k_bf, v_bf, bias, n_batch, t_len, tq=256, ts=512):
    nq, ns = t_len // tq, t_len // ts
    last = lambda i: (i * tq + tq - 1) // ts
    kern = functools.partial(_attn_prompt_kernel, tq=tq, ts=ts)
    return pl.pallas_call(
        kern,
        grid=(n_batch, nq, ns),
        in_specs=[pl.BlockSpec((tq, WA), lambda b, i, j: (b * nq + i, 0)),
                  pl.BlockSpec((ts, WA), lambda b, i, j: (b * ns + jnp.minimum(j, last(i)), 0)),
                  pl.BlockSpec((ts, 2 * WA), lambda b, i, j: (b * ns + jnp.minimum(j, last(i)), 0)),
                  pl.BlockSpec((tq, ts), lambda b, i, j: (b * nq + i, jnp.minimum(j, last(i))))],
        out_specs=pl.BlockSpec((tq, WA), lambda b, i, j: (b * nq + i, 0)),
        out_shape=jax.ShapeDtypeStruct((n_batch * t_len, WA), F32),
        scratch_shapes=[pltpu.VMEM((H_A, tq, LANES), F32), pltpu.VMEM((H_A, tq, LANES), F32),
                        pltpu.VMEM((tq, WA), F32)],
        compiler_params=_cparams(3),
        name="attn_prompt",
    )(q_bf, k_bf, v_bf, bias)


def _log_sigmoid(x):
    return jnp.minimum(x, 0.0) - jnp.log1p(jnp.exp(-jnp.abs(x)))


def _bf16_split3(x):
    x1 = x.astype(BF16)
    r1 = x - x1.astype(F32)
    x2 = r1.astype(BF16)
    x3 = (r1 - x2.astype(F32)).astype(BF16)
    return x1, x2, x3


def _gla_kernel(q_ref, k_ref, v_ref, tail_ref, wg_ref, bg_ref, s0_ref, o_ref, sfin_ref,
                st_ref, cum_all, kk_all, xs_all, *, tb, chunk, sub):
    c_id = pl.program_id(1)
    n_sub = chunk // sub
    nt = (((1,), (1,)), ((), ()))
    tn = (((0,), (0,)), ((), ()))

    @pl.when(c_id == 0)
    def _():
        for h in range(H_B):
            st_ref[h] = s0_ref[h].T

    rows = lax.broadcasted_iota(I32, (chunk, LANES), 0)
    cols = lax.broadcasted_iota(I32, (chunk, LANES), 1)
    tri = (lax.broadcasted_iota(I32, (chunk, chunk), 1)
           <= lax.broadcasted_iota(I32, (chunk, chunk), 0)).astype(BF16)
    ones = jnp.ones((LANES, LANES), BF16)

    def load(ref, r0, n_rows, c0, width):
        x = ref[pl.ds(r0, n_rows), c0:c0 + width]
        if n_rows < chunk:
            x = jnp.concatenate([x, jnp.zeros((chunk - n_rows, width), x.dtype)], axis=0)
        return x

    n_rows = min(tb, chunk)
    n_chunks = max(tb // chunk, 1)
    items = [(c, h) for c in range(n_chunks) for h in range(H_B)]
    slot_of = {it: i for i, it in enumerate(items)}
    lane_hit = [(cols % sub == sl) & (cols // sub == rows // sub) & (cols < chunk)
                for sl in range(sub)]
    row_blk = rows // sub

    ab = [load(tail_ref, c * chunk, n_rows, 0, N_TAIL)[:, L_AB:L_AB + GATE_RANK].astype(BF16)
          for c in range(n_chunks)]
    wg = [wg_ref[:, h * DK_B:(h + 1) * DK_B].astype(BF16) for h in range(H_B)]
    q, k, vb, la = {}, {}, {}, {}
    for c, h in items:
        q[c, h] = load(q_ref, c * chunk, n_rows, h * DK_B, DK_B) * (DK_B ** -0.5)
        k[c, h] = load(k_ref, c * chunk, n_rows, h * DK_B, DK_B)
        vb[c, h] = load(v_ref, c * chunk, n_rows, h * DV_B, DV_B).astype(BF16)
        x = jnp.dot(ab[c], wg[h], preferred_element_type=F32) + bg_ref[:, h * DK_B:(h + 1) * DK_B]
        la[c, h] = _log_sigmoid(x) / GATE_TAU
        if n_rows < chunk:
            la[c, h] = jnp.where(rows < n_rows, la[c, h], 0.0)

    cum = {}
    for it in items:
        l1, l2, l3 = _bf16_split3(la[it])
        cum[it] = (jnp.dot(tri, l1, preferred_element_type=F32)
                   + jnp.dot(tri, l2, preferred_element_type=F32)
                   + jnp.dot(tri, l3, preferred_element_type=F32))
        cum_all[slot_of[it]] = cum[it]
        kk_all[slot_of[it]] = k[it]

    att = {}
    for it in items:
        cum_ref = cum_all.at[slot_of[it]]
        a = jnp.zeros((chunk, LANES), F32)
        for i in range(1, n_sub):
            r_i = cum_ref[i * sub - 1:i * sub, :]
            qt = (q[it] * jnp.exp(jnp.minimum(cum[it] - r_i, 0.0))).astype(BF16)
            kt = (k[it] * jnp.exp(jnp.minimum(r_i - cum[it], 0.0))).astype(BF16)
            a_i = lax.dot_general(qt, kt, nt, preferred_element_type=F32)
            if chunk < LANES:
                a_i = jnp.concatenate([a_i, jnp.zeros((chunk, LANES - chunk), F32)], axis=1)
            a = jnp.where((row_blk == i) & (cols < i * sub), a_i, a)
        att[it] = a

    for it in items:
        cum_ref, kk_ref, xs_ref = (r.at[slot_of[it]] for r in (cum_all, kk_all, xs_all))
        for sl in range(sub):
            cum_s = jnp.concatenate(
                [jnp.broadcast_to(cum_ref[i * sub + sl:i * sub + sl + 1, :], (sub, LANES))
                 for i in range(n_sub)], axis=0)
            k_s = jnp.concatenate(
                [jnp.broadcast_to(kk_ref[i * sub + sl:i * sub + sl + 1, :], (sub, LANES))
                 for i in range(n_sub)], axis=0)
            dec = jnp.where(rows % sub >= sl, jnp.exp(jnp.minimum(cum[it] - cum_s, 0.0)), 0.0)
            xs_ref[sl * chunk:(sl + 1) * chunk, :] = (q[it] * k_s * dec).astype(BF16)
    n_x = sub * chunk
    red = jnp.dot(xs_all[...].reshape(len(items) * n_x, LANES), ones, preferred_element_type=F32)

    o_intra, upd, qdec, sdec = {}, {}, {}, {}
    for it in items:
        a = att[it]
        base = slot_of[it] * n_x
        for sl in range(sub):
            a = jnp.where(lane_hit[sl], red[base + sl * chunk:base + (sl + 1) * chunk, :], a)
        o_intra[it] = jnp.dot(a[:, :chunk].astype(BF16), vb[it], preferred_element_type=F32)
        last = cum_all[slot_of[it], chunk - 1:chunk, :]
        kd = (k[it] * jnp.exp(last - cum[it])).astype(BF16)
        upd[it] = lax.dot_general(vb[it], kd, tn, preferred_element_type=F32)
        qdec[it] = (q[it] * jnp.exp(cum[it])).astype(BF16)
        sdec[it] = jnp.exp(last)

    for h in range(H_B):
        st = st_ref[h]
        for c in range(n_chunks):
            o = o_intra[c, h] + lax.dot_general(qdec[c, h], st.astype(BF16), nt,
                                                preferred_element_type=F32)
            o_ref[c * chunk:c * chunk + n_rows, h * DV_B:(h + 1) * DV_B] = o[:n_rows]
            st = st * sdec[c, h] + upd[c, h]
        st_ref[h] = st

    @pl.when(c_id == pl.num_programs(1) - 1)
    def _():
        for h in range(H_B):
            sfin_ref[h] = st_ref[h].T


def _gla(z, tail, w_gate_up, b_gate, s0, n_batch, t_len, tb, chunk, sub):
    nb = t_len // tb
    n_slots = max(tb // chunk, 1) * H_B
    kern = functools.partial(_gla_kernel, tb=tb, chunk=chunk, sub=sub)
    return pl.pallas_call(
        kern,
        grid=(n_batch, nb),
        in_specs=[pl.BlockSpec((tb, WKB), lambda b, c: (b * nb + c, C_QB // WKB)),
                  pl.BlockSpec((tb, WKB), lambda b, c: (b * nb + c, C_KB // WKB)),
                  pl.BlockSpec((tb, WVB), lambda b, c: (b * nb + c, C_VB // WVB)),
                  pl.BlockSpec((tb, N_TAIL), lambda b, c: (b * nb + c, 0)),
                  pl.BlockSpec((GATE_RANK, WKB), lambda b, c: (0, 0)),
                  pl.BlockSpec((1, WKB), lambda b, c: (0, 0)),
                  pl.BlockSpec((None, H_B, DK_B, DV_B), lambda b, c: (b, 0, 0, 0))],
        out_specs=[pl.BlockSpec((tb, WVB), lambda b, c: (b * nb + c, 0)),
                   pl.BlockSpec((None, H_B, DK_B, DV_B), lambda b, c: (b, 0, 0, 0))],
        out_shape=[jax.ShapeDtypeStruct((n_batch * t_len, WVB), F32),
                   jax.ShapeDtypeStruct((n_batch, H_B, DK_B, DV_B), F32)],
        scratch_shapes=[pltpu.VMEM((H_B, DV_B, DK_B), F32), pltpu.VMEM((n_slots, chunk, LANES), F32),
                        pltpu.VMEM((n_slots, chunk, LANES), F32),
                        pltpu.VMEM((n_slots, sub * chunk, LANES), BF16)],
        compiler_params=_cparams(2),
        name="gla",
    )(z, z, z, tail, w_gate_up, b_gate, s0)


def _silu(x):
    return x / (1.0 + jnp.exp(-x))


def _merge_kernel(att_ref, ga_ref, ob_ref, gb_ref, gn_ref, wo_ref, x_ref, nf_ref, o_ref):
    a = (att_ref[...] * _silu(ga_ref[...])).astype(BF16)
    y = jnp.dot(a, wo_ref[:WA, :], preferred_element_type=F32)
    gn = gn_ref[...]
    for h in range(H_B):
        sl = slice(h * DV_B, (h + 1) * DV_B)
        ob = ob_ref[:, sl]
        bn = ob * lax.rsqrt(jnp.mean(ob * ob, axis=-1, keepdims=True) + EPS) * gn
        bp = (bn * _silu(gb_ref[:, sl])).astype(BF16)
        y = y + jnp.dot(bp, wo_ref[WA + h * DV_B:WA + (h + 1) * DV_B, :], preferred_element_type=F32)
    xo = x_ref[...] + y
    o_ref[...] = xo * lax.rsqrt(jnp.mean(xo * xo, axis=-1, keepdims=True) + EPS) * nf_ref[...]


def _merge(att, z, ob, gla_norm, w_out_bf, x2d, norm_f, tm):
    m = x2d.shape[0]
    return pl.pallas_call(
        _merge_kernel,
        grid=(m // tm,),
        in_specs=[pl.BlockSpec((tm, WA), lambda i: (i, 0)),
                  pl.BlockSpec((tm, WA), lambda i: (i, C_GA // WA)),
                  pl.BlockSpec((tm, WVB), lambda i: (i, 0)),
                  pl.BlockSpec((tm, WVB), lambda i: (i, C_GB // WVB)),
                  pl.BlockSpec((1, DV_B), lambda i: (0, 0)),
                  pl.BlockSpec((WA + WVB, D_MODEL), lambda i: (0, 0)),
                  pl.BlockSpec((tm, D_MODEL), lambda i: (i, 0)),
                  pl.BlockSpec((1, D_MODEL), lambda i: (0, 0))],
        out_specs=pl.BlockSpec((tm, D_MODEL), lambda i: (i, 0)),
        out_shape=jax.ShapeDtypeStruct((m, D_MODEL), F32),
        compiler_params=_cparams(1),
        name="merge",
    )(att, z, ob, z, gla_norm, w_out_bf, x2d, norm_f)


PAGES_PER_STEP = 16


def _idx_pages_kernel(pt_ref, q_ref, w_ref, *refs, n_pages, t_dec, causal):
    page_refs, o_ref = refs[:n_pages], refs[n_pages]
    q = q_ref[...]
    w = w_ref[...]
    for p in range(n_pages):
        kp = page_refs[p][...].astype(BF16)
        d = jnp.dot(q, kp, preferred_element_type=F32)
        r = w * jnp.maximum(d, 0.0)
        acc = r[0:t_dec]
        for h in range(1, H_IDX):
            acc = acc + r[h * t_dec:(h + 1) * t_dec]
        if causal:
            t_i = lax.broadcasted_iota(I32, acc.shape, 0)
            s_i = lax.broadcasted_iota(I32, acc.shape, 1)
            acc = jnp.where(s_i <= t_i, acc, -jnp.inf)
        o_ref[:, p * PAGE_SIZE:(p + 1) * PAGE_SIZE] = acc


def _idx_sample_past(page_table, q_stk, w_stk, pool_ik, n_seq, t_dec, n_pages):
    pps = PAGES_PER_STEP
    n_g = n_pages // pps
    rows = H_IDX * t_dec
    kern = functools.partial(_idx_pages_kernel, n_pages=pps, t_dec=t_dec, causal=False)
    page_spec = lambda r: pl.BlockSpec(
        (None, D_IDX, PAGE_SIZE), lambda b, g, pt, r=r: (pt[b * n_pages + g * pps + r], 0, 0))
    return pl.pallas_call(
        kern,
        grid_spec=pltpu.PrefetchScalarGridSpec(
            num_scalar_prefetch=1,
            grid=(n_seq, n_g),
            in_specs=[pl.BlockSpec((rows, D_IDX), lambda b, g, pt: (b, 0)),
                      pl.BlockSpec((rows, 1), lambda b, g, pt: (b, 0))]
                     + [page_spec(r) for r in range(pps)],
            out_specs=pl.BlockSpec((t_dec, pps * PAGE_SIZE), lambda b, g, pt: (b, g))),
        out_shape=jax.ShapeDtypeStruct((n_seq * t_dec, n_pages * PAGE_SIZE), F32),
        compiler_params=_cparams(2),
        name="idx_sample_past",
    )(page_table.reshape(-1), q_stk, w_stk, *([pool_ik] * pps))


def _idx_sample_new(q_stk, w_stk, ki_new_pad, n_seq, t_dec):
    rows = H_IDX * t_dec
    kern = functools.partial(_idx_pages_kernel, None, n_pages=1, t_dec=t_dec, causal=True)
    return pl.pallas_call(
        kern,
        grid=(n_seq,),
        in_specs=[pl.BlockSpec((rows, D_IDX), lambda b: (b, 0)),
                  pl.BlockSpec((rows, 1), lambda b: (b, 0)),
                  pl.BlockSpec((None, D_IDX, PAGE_SIZE), lambda b: (b, 0, 0))],
        out_specs=pl.BlockSpec((t_dec, PAGE_SIZE), lambda b: (b, 0)),
        out_shape=jax.ShapeDtypeStruct((n_seq * t_dec, PAGE_SIZE), F32),
        compiler_params=_cparams(1),
        name="idx_sample_new",
    )(q_stk, w_stk, ki_new_pad)


ATTN_PAGES_PER_STEP = 8


def _attn_sample_kernel(pt_ref, q_ref, sel_ref, selnew_ref, e_ref, knew_ref, vnew_ref, *refs,
                        n_pages, t_dec):
    k_refs, v_refs = refs[:n_pages], refs[n_pages:2 * n_pages]
    o_ref, m_ref, l_ref, acc_ref = refs[2 * n_pages:]
    _attn_sample_body(pl.program_id(1), pl.num_programs(1), q_ref, sel_ref, selnew_ref, e_ref,
                      knew_ref, vnew_ref, k_refs, v_refs, o_ref, m_ref, l_ref, acc_ref, t_dec)


def _attn_sample_body(g, n_g, q_ref, sel_ref, selnew_ref, e_ref, knew_ref, vnew_ref, k_refs, v_refs,
                      o_ref, m_ref, l_ref, acc_ref, t_dec):
    n_pages = len(k_refs)
    rows = H_A * t_dec
    nt = (((1,), (1,)), ((), ()))
    q = q_ref[...]
    n_col = PAGE_SIZE * H_A
    head_hot = (lax.broadcasted_iota(I32, (rows, LANES), 0) // t_dec
                == lax.broadcasted_iota(I32, (rows, LANES), 1)).astype(BF16)

    def update(sels, k_pages, v_pages):
        lhs = jnp.concatenate(
            [jnp.concatenate([jnp.tile(sel, (H_A, 1)).astype(BF16), head_hot], axis=1) for sel in sels],
            axis=0)
        bias = jnp.dot(lhs, e_ref[...], preferred_element_type=F32)
        scores = []
        m_next = m_ref[...]
        for i, k_page in enumerate(k_pages):
            s = lax.dot_general(q, k_page[...].astype(BF16), nt, preferred_element_type=F32)
            s = s + bias[i * rows:(i + 1) * rows]
            m_next = jnp.maximum(m_next, jnp.max(s, axis=1, keepdims=True))
            scores.append(s)
        alpha = jnp.exp2(m_ref[...] - m_next)
        m_rep = jnp.tile(m_next, (1, n_col // LANES))
        l_new = alpha * l_ref[...]
        acc = acc_ref[...] * alpha
        for s, v_page in zip(scores, v_pages):
            p = jnp.exp2(s - m_rep)
            l_new = l_new + jnp.sum(p, axis=1, keepdims=True)
            acc = acc + jnp.dot(p.astype(BF16), v_page[...].astype(BF16), preferred_element_type=F32)
        m_ref[...] = m_next
        l_ref[...] = l_new
        acc_ref[...] = acc

    @pl.when(g == 0)
    def _():
        m_ref[...] = jnp.full(m_ref.shape, NEG_BIG, F32)
        l_ref[...] = jnp.zeros(l_ref.shape, F32)
        acc_ref[...] = jnp.zeros(acc_ref.shape, F32)
        update([selnew_ref[...]], [knew_ref], [vnew_ref])

    update([sel_ref[:, p_i * PAGE_SIZE:(p_i + 1) * PAGE_SIZE] for p_i in range(n_pages)],
           k_refs, v_refs)

    @pl.when(g == n_g - 1)
    def _():
        o_ref[...] = acc_ref[...] / l_ref[...]


def _proj_attn_kernel(pt_ref, x_ref, g_ref, wa_ref, wb_ref, q_ref, sel_ref, selnew_ref, e_ref,
                      knew_ref, vnew_ref, *refs, n_pages, t_dec, n_a, n_j, n_g):
    k_refs, v_refs = refs[:n_pages], refs[n_pages:2 * n_pages]
    z_ref, att_ref, h_ref, m_ref, l_ref, acc_ref = refs[2 * n_pages:]
    step = pl.program_id(0)
    _proj_body(step % n_j, x_ref, g_ref, wa_ref, wb_ref, z_ref, h_ref, n_a)
    _attn_sample_body(step % n_g, n_g, q_ref, sel_ref, selnew_ref, e_ref, knew_ref, vnew_ref,
                      k_refs, v_refs, att_ref, m_ref, l_ref, acc_ref, t_dec)


def _proj_attn(x2d, g, w_a, n_cols_a, w_b, tm, tn,
               page_table, q_stk, sel_past, sel_new, expand, knew_pad, vnew_pad, pool_k, pool_v,
               n_seq, t_dec, n_pages):
    m, k = x2d.shape
    n_a, n_b = n_cols_a // tn, w_b.shape[1] // tn
    n_j = n_a + n_b
    pps = ATTN_PAGES_PER_STEP
    n_g = n_pages // pps
    n_steps = (m // tm) * n_j
    assert n_steps == n_seq * n_g
    rows = H_A * t_dec
    n_col = PAGE_SIZE * H_A
    kern = functools.partial(_proj_attn_kernel, n_pages=pps, t_dec=t_dec, n_a=n_a, n_j=n_j, n_g=n_g)
    page_spec = lambda r: pl.BlockSpec(
        (None, n_col, DH_A), lambda s, pt, r=r: (pt[(s // n_g) * n_pages + (s % n_g) * pps + r], 0, 0))
    return pl.pallas_call(
        kern,
        grid_spec=pltpu.PrefetchScalarGridSpec(
            num_scalar_prefetch=1,
            grid=(n_steps,),
            in_specs=[pl.BlockSpec((tm, k), lambda s, pt: (s // n_j, 0)),
                      pl.BlockSpec((1, k), lambda s, pt: (0, 0)),
                      pl.BlockSpec((k, tn), lambda s, pt: (0, jnp.minimum(s % n_j, n_a - 1))),
                      pl.BlockSpec((k, tn), lambda s, pt: (0, jnp.maximum(s % n_j - n_a, 0))),
                      pl.BlockSpec((rows, DH_A), lambda s, pt: (s // n_g, 0)),
                      pl.BlockSpec((t_dec, pps * PAGE_SIZE), lambda s, pt: (s // n_g, s % n_g)),
                      pl.BlockSpec((t_dec, PAGE_SIZE), lambda s, pt: (s // n_g, 0)),
                      pl.BlockSpec((2 * PAGE_SIZE, n_col), lambda s, pt: (0, 0)),
                      pl.BlockSpec((None, n_col, DH_A), lambda s, pt: (s // n_g, 0, 0)),
                      pl.BlockSpec((None, n_col, DH_A), lambda s, pt: (s // n_g, 0, 0))]
                     + [page_spec(r) for r in range(pps)] * 2,
            out_specs=[pl.BlockSpec((tm, tn), lambda s, pt: (s // n_j, s % n_j)),
                       pl.BlockSpec((rows, DH_A), lambda s, pt: (s // n_g, 0))],
            scratch_shapes=[pltpu.VMEM((tm, k), BF16),
                            pltpu.VMEM((rows, LANES), F32), pltpu.VMEM((rows, LANES), F32),
                            pltpu.VMEM((rows, DH_A), F32)]),
        out_shape=[jax.ShapeDtypeStruct((m, n_j * tn), F32),
                   jax.ShapeDtypeStruct((n_seq * rows, DH_A), F32)],
        compiler_params=pltpu.CompilerParams(dimension_semantics=("arbitrary",),
                                             vmem_limit_bytes=FUSED_VMEM_LIMIT),
        name="proj_attn",
    )(page_table.reshape(-1), x2d, g, w_a, w_b, q_stk, sel_past, sel_new, expand, knew_pad, vnew_pad,
      *([pool_k] * pps), *([pool_v] * pps))


def _attn_sample(page_table, q_stk, sel_past, sel_new, expand, knew_pad, vnew_pad, pool_k, pool_v,
                 n_seq, t_dec, n_pages):
    pps = ATTN_PAGES_PER_STEP
    n_g = n_pages // pps
    rows = H_A * t_dec
    n_col = PAGE_SIZE * H_A
    kern = functools.partial(_attn_sample_kernel, n_pages=pps, t_dec=t_dec)
    page_spec = lambda r: pl.BlockSpec(
        (None, n_col, DH_A), lambda b, g, pt, r=r: (pt[b * n_pages + g * pps + r], 0, 0))
    return pl.pallas_call(
        kern,
        grid_spec=pltpu.PrefetchScalarGridSpec(
            num_scalar_prefetch=1,
            grid=(n_seq, n_g),
            in_specs=[pl.BlockSpec((rows, DH_A), lambda b, g, pt: (b, 0)),
                      pl.BlockSpec((t_dec, pps * PAGE_SIZE), lambda b, g, pt: (b, g)),
                      pl.BlockSpec((t_dec, PAGE_SIZE), lambda b, g, pt: (b, 0)),
                      pl.BlockSpec((2 * PAGE_SIZE, n_col), lambda b, g, pt: (0, 0)),
                      pl.BlockSpec((None, n_col, DH_A), lambda b, g, pt: (b, 0, 0)),
                      pl.BlockSpec((None, n_col, DH_A), lambda b, g, pt: (b, 0, 0))]
                     + [page_spec(r) for r in range(pps)] * 2,
            out_specs=pl.BlockSpec((rows, DH_A), lambda b, g, pt: (b, 0)),
            scratch_shapes=[pltpu.VMEM((rows, LANES), F32), pltpu.VMEM((rows, LANES), F32),
                            pltpu.VMEM((rows, DH_A), F32)]),
        out_shape=jax.ShapeDtypeStruct((n_seq * rows, DH_A), F32),
        compiler_params=_cparams(2),
        name="attn_sample",
    )(page_table.reshape(-1), q_stk, sel_past, sel_new, expand, knew_pad, vnew_pad,
      *([pool_k] * pps), *([pool_v] * pps))


def _group_common(x2d, norm_in, w_parts, pos, tm, tn, tr, z=None):
    w_full, w_gla, w_tail = w_parts
    if z is None:
        z = _proj(x2d, norm_in, w_full, C_QB, w_gla, tm, tn)
    tail = _proj(x2d, norm_in, w_tail, N_TAIL, None, tm, N_TAIL)
    tab = _rope_tables(pos)
    return z, tail, _rope(z, tail, tab, tr)


def kernel(x_prompt, x_sample, cache_k, cache_v, cache_idx_k, state_gla, page_table,
           norm_in, w_in, w_gate_up, b_gate, gla_norm, w_out, norm_f):
    n_p, t_p, _ = x_prompt.shape
    n_s, t_s, _ = x_sample.shape
    n_pages = page_table.shape[1]
    past = n_pages * PAGE_SIZE
    assert cache_k.shape[0] == 1, "single layer"

    w_parts = _prep_w_in(w_in[0])
    w_out_bf = w_out[0].astype(BF16)
    g_in, g_gla, g_f = norm_in[0][None], gla_norm[0][None], norm_f[None]
    wg, bg = w_gate_up[0], b_gate[0][None]

    m_s = n_s * t_s
    pos_s = jnp.tile(past + jnp.arange(t_s), n_s)
    xs2d = x_sample.reshape(m_s, D_MODEL)
    zs, tails, (newk_s, newv_s, newik_s, q_s, k_s, v_s, qi_s, kie_s, kio_s) = _group_common(
        xs2d, g_in, w_parts, pos_s, m_s, 1024, m_s)
    qi_stk = qi_s.reshape(n_s, t_s, H_IDX, D_IDX).transpose(0, 2, 1, 3).reshape(n_s * H_IDX * t_s, D_IDX)
    w_stk = (tails[:, L_WI:L_WI + H_IDX] * IDX_W_SCALE).reshape(n_s, t_s, H_IDX).transpose(0, 2, 1)
    w_stk = w_stk.reshape(n_s * H_IDX * t_s, 1)
    q_stk = q_s.reshape(n_s, t_s, H_A, DH_A).transpose(0, 2, 1, 3).reshape(n_s * H_A * t_s, DH_A)
    pad_rows = lambda a, n: jnp.pad(a, ((0, 0), (0, n - a.shape[1]), (0, 0)))
    ki_new_pad = jnp.pad(newik_s.reshape(n_s, t_s, D_IDX).transpose(0, 2, 1),
                         ((0, 0), (0, 0), (0, PAGE_SIZE - t_s)))
    knew_pad = pad_rows(newk_s.reshape(n_s, t_s * H_A, DH_A), PAGE_SIZE * H_A)
    vnew_pad = pad_rows(newv_s.reshape(n_s, t_s * H_A, DH_A), PAGE_SIZE * H_A)
    pool_ik = jnp.swapaxes(cache_idx_k[0], 1, 2)
    pool_k = cache_k[0].reshape(-1, PAGE_SIZE * H_A, DH_A)
    pool_v = cache_v[0].reshape(-1, PAGE_SIZE * H_A, DH_A)

    sc_past = _idx_sample_past(page_table, qi_stk, w_stk, pool_ik, n_s, t_s, n_pages)
    sc_new = _idx_sample_new(qi_stk, w_stk, ki_new_pad, n_s, t_s)
    sc_s = jnp.concatenate([sc_past, sc_new], axis=1)
    k_top_s = min(TOPK_MAX, (past + t_s) // 4)
    n_tiles_s = sc_s.shape[1] // LANES
    sub_s = max(d for d in range(1, 6) if n_tiles_s % d == 0)
    bias_s = _select(sc_s, k_top_s, sub_s, 0, F32)
    col = jnp.arange(PAGE_SIZE * H_A)[None, :]
    row = jnp.arange(2 * PAGE_SIZE)[:, None]
    expand = jnp.where(row < PAGE_SIZE, (col // H_A == row).astype(F32),
                       jnp.where((row < PAGE_SIZE + H_A) & (col % H_A != row - PAGE_SIZE), NEG_BIG, 0.0)
                       ).astype(BF16)
    attn_args = (page_table, q_stk, bias_s[:, :past], bias_s[:, past:], expand,
                 knew_pad, vnew_pad, pool_k, pool_v, n_s, t_s, n_pages)

    xp2d = x_prompt.reshape(n_p * t_p, D_MODEL)
    tm_p, tn_p = 512, 512
    fused_steps = (xp2d.shape[0] // tm_p) * (N_MAIN // tn_p)
    if fused_steps == n_s * (n_pages // ATTN_PAGES_PER_STEP):
        zp, att_stk = _proj_attn(xp2d, g_in, w_parts[0], C_QB, w_parts[1], tm_p, tn_p, *attn_args)
    else:
        zp, att_stk = None, _attn_sample(*attn_args)
    att_s = att_stk.reshape(n_s, H_A, t_s, DH_A).transpose(0, 2, 1, 3).reshape(m_s, WA)

    zp, tailp, (newk_p, newv_p, newik_p, q_p, k_p, v_p, qi_p, kie_p, kio_p) = _group_common(
        xp2d, g_in, w_parts, jnp.arange(t_p), 1024, 512, 256, z=zp)
    sc_p = _idx_prompt(qi_p, tailp, kie_p, kio_p, n_p, t_p)
    n_groups_p = max(1, min(4, t_p // 1024))
    bias_p = _select(sc_p, min(TOPK_MAX, t_p // 4), 4, t_p, BF16, n_half=2, n_groups=n_groups_p)
    att_p = _attn_prompt(q_p, k_p, v_p, bias_p, n_p, t_p)
    s0_p = jnp.zeros((n_p, H_B, DK_B, DV_B), F32)
    ob_p, sfin_p = _gla(zp, tailp, wg, bg, s0_p, n_p, t_p, 256, 64, 16)
    y_p = _merge(att_p, zp, ob_p, g_gla, w_out_bf, xp2d, g_f, 256)

    ob_s, sfin_s = _gla(zs, tails, wg, bg, state_gla[0], n_s, t_s, t_s, 16, 16)
    y_s = _merge(att_s, zs, ob_s, g_gla, w_out_bf, xs2d, g_f, m_s)

    return (y_p.reshape(n_p, t_p, D_MODEL), y_s.reshape(n_s, t_s, D_MODEL),
            newk_p.reshape(1, n_p, t_p, H_A, DH_A), newv_p.reshape(1, n_p, t_p, H_A, DH_A),
            newik_p.reshape(1, n_p, t_p, D_IDX), sfin_p[None],
            newk_s.reshape(1, n_s, t_s, H_A, DH_A), newv_s.reshape(1, n_s, t_s, H_A, DH_A),
            newik_s.reshape(1, n_s, t_s, D_IDX), sfin_s[None])
```

```python
import functools
import math

import jax
import jax.numpy as jnp
import numpy as np
from jax import lax
from jax.experimental import pallas as pl
from jax.experimental.pallas import tpu as pltpu

F32 = jnp.float32
BF16 = jnp.bfloat16
I32 = jnp.int32

D_MODEL = 2048
PAGE_SIZE = 128
DH_A = 128
H_A = 8
H_IDX = 16
D_IDX = 64
TOPK_MAX = 256
DK_B = 128
DV_B = 256
H_B = 4
GATE_RANK = 16
GATE_TAU = 16.0
ROPE_THETA = 10000.0
EPS = 1e-6
NEG_BIG = -1e30
WA = H_A * DH_A
WI = H_IDX * D_IDX
WKB = H_B * DK_B
WVB = H_B * DV_B

C_QA, C_KA, C_VA, C_GA, C_QI = 0, WA, 2 * WA, 3 * WA, 4 * WA
C_QB = C_QI + WI
C_KB = C_QB + WKB
C_VB = C_KB + WKB
C_GB = C_VB + WVB
N_MAIN = C_GB + WVB
L_KI, L_WI, L_AB = 0, D_IDX, D_IDX + H_IDX
N_TAIL = 128

LANES = 128
INT_MIN = -2 ** 31
VMEM_LIMIT = 48 * 1024 * 1024
FUSED_VMEM_LIMIT = 56 * 1024 * 1024


def _prep_w_in(w_in):
    o_ki = C_QI + WI
    o_qb = o_ki + D_IDX + H_IDX
    o_ab = o_qb + 2 * WKB + 2 * WVB
    w_bf = w_in.astype(BF16)
    pad = jnp.zeros((w_in.shape[0], N_TAIL - D_IDX - H_IDX - GATE_RANK), BF16)
    tail = jnp.concatenate([w_bf[:, o_ki:o_qb], w_bf[:, o_ab:], pad], axis=1)
    return w_bf, w_bf[:, o_qb:o_ab], tail


def _cparams(n_axes):
    return pltpu.CompilerParams(dimension_semantics=("arbitrary",) * n_axes,
                                vmem_limit_bytes=VMEM_LIMIT)


def _proj_kernel(x_ref, g_ref, wa_ref, wb_ref, o_ref, h_ref, *, n_a):
    _proj_body(pl.program_id(1), x_ref, g_ref, wa_ref, wb_ref, o_ref, h_ref, n_a)


def _proj_body(j, x_ref, g_ref, wa_ref, wb_ref, o_ref, h_ref, n_a):
    @pl.when(j == 0)
    def _():
        x = x_ref[...]
        ms = jnp.mean(x * x, axis=-1, keepdims=True)
        h_ref[...] = (x * lax.rsqrt(ms + EPS) * g_ref[...]).astype(BF16)

    w = jnp.where(j < n_a, wa_ref[...], wb_ref[...])
    o_ref[...] = jnp.dot(h_ref[...], w, preferred_element_type=F32)


def _proj(x2d, g, w_a, n_cols_a, w_b, tm, tn):
    m, k = x2d.shape
    n_a = n_cols_a // tn
    if w_b is None:
        w_b, n_b = w_a, 0
    else:
        n_b = w_b.shape[1] // tn
    return pl.pallas_call(
        functools.partial(_proj_kernel, n_a=n_a),
        grid=(m // tm, n_a + n_b),
        in_specs=[pl.BlockSpec((tm, k), lambda i, j: (i, 0)),
                  pl.BlockSpec((1, k), lambda i, j: (0, 0)),
                  pl.BlockSpec((k, tn), lambda i, j: (0, jnp.minimum(j, n_a - 1))),
                  pl.BlockSpec((k, tn), lambda i, j: (0, jnp.maximum(j - n_a, 0)))],
        out_specs=pl.BlockSpec((tm, tn), lambda i, j: (i, j)),
        out_shape=jax.ShapeDtypeStruct((m, (n_a + n_b) * tn), F32),
        scratch_shapes=[pltpu.VMEM((tm, k), BF16)],
        compiler_params=_cparams(2),
        name="proj",
    )(x2d, g, w_a, w_b)


def _rope128(x, cos, sin_signed):
    return x * cos + pltpu.roll(x, 64, axis=1) * sin_signed


def _rope64(x, cos, sin_lo, sin_hi):
    return x * cos + pltpu.roll(x, 96, axis=1) * sin_lo + pltpu.roll(x, 32, axis=1) * sin_hi


def _rope_kernel(qa_ref, ka_ref, va_ref, qi_ref, tail_ref, tab_ref,
                 newk_ref, newv_ref, newik_ref, qbf_ref, kbf_ref, vbf_ref, qibf_ref,
                 kie_ref, kio_ref):
    cos_a, sin_a = tab_ref[0], tab_ref[1]
    cos_i, sin_lo, sin_hi = tab_ref[2], tab_ref[3], tab_ref[4]
    scale = (DH_A ** -0.5) * math.log2(math.e)
    for h in range(H_A):
        sl = slice(h * DH_A, (h + 1) * DH_A)
        q = _rope128(qa_ref[:, sl], cos_a, sin_a)
        qbf_ref[:, sl] = (q * scale).astype(BF16)
        k = _rope128(ka_ref[:, sl], cos_a, sin_a)
        newk_ref[:, sl] = k
        kbf_ref[:, sl] = k.astype(BF16)
    v = va_ref[...]
    newv_ref[...] = v
    ones = jnp.ones((v.shape[0], DH_A), BF16)
    for h in range(H_A):
        vbf_ref[:, 2 * h * DH_A:(2 * h + 1) * DH_A] = v[:, h * DH_A:(h + 1) * DH_A].astype(BF16)
        vbf_ref[:, (2 * h + 1) * DH_A:(2 * h + 2) * DH_A] = ones
    for j in range(WI // LANES):
        sl = slice(j * LANES, (j + 1) * LANES)
        qibf_ref[:, sl] = _rope64(qi_ref[:, sl], cos_i, sin_lo, sin_hi).astype(BF16)
    ki = _rope64(tail_ref[...], cos_i, sin_lo, sin_hi)
    newik_ref[...] = ki[:, L_KI:L_KI + D_IDX]
    lane = lax.broadcasted_iota(I32, ki.shape, 1)
    ki_lo = jnp.where(lane < D_IDX, ki, 0.0)
    kie_ref[...] = ki_lo.astype(BF16)
    kio_ref[...] = pltpu.roll(ki_lo, D_IDX, axis=1).astype(BF16)


def _rope(z, tail, tab, tm):
    m = z.shape[0]
    n_tab = tab.shape[1] // tm
    col = lambda c: pl.BlockSpec((tm, WA), lambda i, c=c: (i, c))
    row = lambda w: pl.BlockSpec((tm, w), lambda i: (i, 0))
    return pl.pallas_call(
        _rope_kernel,
        grid=(m // tm,),
        in_specs=[col(C_QA // WA), col(C_KA // WA), col(C_VA // WA), col(C_QI // WA),
                  row(N_TAIL),
                  pl.BlockSpec((5, tm, LANES), lambda i: (0, i % n_tab, 0))],
        out_specs=[row(WA), row(WA), row(D_IDX), row(WA), row(WA), row(2 * WA), row(WI),
                   row(LANES), row(LANES)],
        out_shape=[jax.ShapeDtypeStruct((m, WA), F32), jax.ShapeDtypeStruct((m, WA), F32),
                   jax.ShapeDtypeStruct((m, D_IDX), F32),
                   jax.ShapeDtypeStruct((m, WA), BF16), jax.ShapeDtypeStruct((m, WA), BF16),
                   jax.ShapeDtypeStruct((m, 2 * WA), BF16), jax.ShapeDtypeStruct((m, WI), BF16),
                   jax.ShapeDtypeStruct((m, LANES), BF16), jax.ShapeDtypeStruct((m, LANES), BF16)],
        compiler_params=_cparams(1),
        name="rope",
    )(z, z, z, z, tail, tab)


def _rope_tables(pos):
    pos = pos.astype(F32)[:, None]
    half_a, half_i = DH_A // 2, D_IDX // 2
    inv_a = ROPE_THETA ** (-jnp.arange(half_a, dtype=F32) / half_a)
    inv_i = ROPE_THETA ** (-jnp.arange(half_i, dtype=F32) / half_i)
    ca, sa = jnp.cos(pos * inv_a), jnp.sin(pos * inv_a)
    ci, si = jnp.cos(pos * inv_i), jnp.sin(pos * inv_i)
    zi = jnp.zeros_like(si)
    return jnp.stack([
        jnp.concatenate([ca, ca], axis=1),
        jnp.concatenate([-sa, sa], axis=1),
        jnp.concatenate([ci, ci, ci, ci], axis=1),
        jnp.concatenate([-si, zi, -si, zi], axis=1),
        jnp.concatenate([zi, si, zi, si], axis=1)])


IDX_W_SCALE = (H_IDX ** -0.5) * (D_IDX ** -0.5)


def _idx_scores(q_ref, w, kie, kio):
    nt = (((1,), (1,)), ((), ()))
    acc = jnp.zeros((q_ref.shape[0], kie.shape[0]), F32)
    for j in range(H_IDX // 2):
        qp = q_ref[:, j * LANES:(j + 1) * LANES]
        de = lax.dot_general(qp, kie, nt, preferred_element_type=F32)
        acc = acc + w[:, 2 * j:2 * j + 1] * jnp.maximum(de, 0.0)
        do = lax.dot_general(qp, kio, nt, preferred_element_type=F32)
        acc = acc + w[:, 2 * j + 1:2 * j + 2] * jnp.maximum(do, 0.0)
    return acc


def _idx_prompt_kernel(q_ref, tail_ref, kie_ref, kio_ref, o_ref, *, tq, tc, t_len):
    i = pl.program_id(1)
    w = tail_ref[:, L_WI:L_WI + H_IDX] * IDX_W_SCALE
    n_c = (i * tq + tq - 1) // tc + 1
    row = i * tq + lax.broadcasted_iota(I32, (tq, tc), 0)

    def body(c, carry):
        c0 = pl.multiple_of(c * tc, tc)
        acc = _idx_scores(q_ref, w, kie_ref[pl.ds(c0, tc), :], kio_ref[pl.ds(c0, tc), :])
        col = c0 + lax.broadcasted_iota(I32, (tq, tc), 1)
        o_ref[:, pl.ds(c0, tc)] = jnp.where(col <= row, acc, -jnp.inf)
        return carry

    lax.fori_loop(0, n_c, body, 0)

    def fill(c, carry):
        c0 = pl.multiple_of(c * tc, tc)
        o_ref[:, pl.ds(c0, tc)] = jnp.full((tq, tc), -jnp.inf, F32)
        return carry

    lax.fori_loop(n_c, t_len // tc, fill, 0)


def _idx_prompt(qi_bf, tail, kie, kio, n_batch, t_len, tq=128, tc=256):
    nq = t_len // tq
    kern = functools.partial(_idx_prompt_kernel, tq=tq, tc=tc, t_len=t_len)
    return pl.pallas_call(
        kern,
        grid=(n_batch, nq),
        in_specs=[pl.BlockSpec((tq, WI), lambda b, i: (b * nq + i, 0)),
                  pl.BlockSpec((tq, N_TAIL), lambda b, i: (b * nq + i, 0)),
                  pl.BlockSpec((t_len, LANES), lambda b, i: (b, 0)),
                  pl.BlockSpec((t_len, LANES), lambda b, i: (b, 0))],
        out_specs=pl.BlockSpec((tq, t_len), lambda b, i: (b * nq + i, 0)),
        out_shape=jax.ShapeDtypeStruct((n_batch * t_len, t_len), F32),
        compiler_params=_cparams(2),
        name="idx_prompt",
    )(qi_bf, tail, kie, kio)


SELECT_STEPS_PER_CHECK = 3
SELECT_MAX_CHECKS = 16
SELECT_ROWS = 128


def _select_body(s_ref, o_ref, *, n_half, sub, n_all, n_c, unroll, k_top):
    tc = sub * LANES
    kf = float(k_top)
    halves = [pl.ds(hh * SELECT_ROWS, SELECT_ROWS) for hh in range(n_half)]
    shape = (SELECT_ROWS, LANES)

    def over_chunks(fn, init, lo_c, hi_c):
        if unroll:
            carry = init
            for c in range(lo_c, hi_c):
                carry = fn(c * tc, carry)
            return carry
        return lax.fori_loop(lo_c, hi_c, lambda c, carry: fn(pl.multiple_of(c * tc, tc), carry), init)

    def stats(c0, carry):
        out = []
        for rows, (mn, mx, cnt) in zip(halves, carry):
            for u in range(sub):
                x = s_ref[rows, pl.ds(c0 + u * LANES, LANES)]
                valid = x > -jnp.inf
                mn = jnp.minimum(mn, jnp.where(valid, x, jnp.inf))
                mx = jnp.maximum(mx, x)
                cnt = cnt + jnp.where(valid, 1.0, 0.0)
            out.append((mn, mx, cnt))
        return tuple(out)

    st = over_chunks(stats, tuple((jnp.full(shape, jnp.inf, F32), jnp.full(shape, -jnp.inf, F32),
                                   jnp.zeros(shape, F32)) for _ in halves), 0, n_c)
    lo0 = tuple(jnp.min(mn, axis=1, keepdims=True) for mn, _, _ in st)
    hi0 = tuple(jnp.max(mx, axis=1, keepdims=True) for _, mx, _ in st)
    c0_lo = tuple(jnp.sum(cnt, axis=1, keepdims=True) for _, _, cnt in st)

    def count_ge(thrs):
        thr_b = [jnp.broadcast_to(t, shape) for t in thrs]

        def body(c0, accs):
            out = []
            for rows, tb, acc in zip(halves, thr_b, accs):
                for u in range(sub):
                    x = s_ref[rows, pl.ds(c0 + u * LANES, LANES)]
                    acc = acc + jnp.where(x >= tb, 1.0, 0.0)
                out.append(acc)
            return tuple(out)

        accs = over_chunks(body, tuple(jnp.zeros(shape, F32) for _ in halves), 0, n_c)
        return [jnp.sum(a, axis=1, keepdims=True) for a in accs]

    def unresolved(state):
        it, _, _, c_lo = state
        worst = c_lo[0]
        for c in c_lo[1:]:
            worst = jnp.maximum(worst, c)
        return jnp.logical_and(it < SELECT_MAX_CHECKS, jnp.max(worst) > kf)

    def refine(state):
        it, lo, hi, c_lo = state
        lo, hi, c_lo = list(lo), list(hi), list(c_lo)
        for _ in range(SELECT_STEPS_PER_CHECK):
            mid = [0.5 * a + 0.5 * b for a, b in zip(lo, hi)]
            cnt = count_ge(mid)
            for i in range(n_half):
                ok = cnt[i] >= kf
                lo[i] = jnp.where(ok, mid[i], lo[i])
                c_lo[i] = jnp.where(ok, cnt[i], c_lo[i])
                hi[i] = jnp.where(ok, hi[i], mid[i])
        return it + 1, tuple(lo), tuple(hi), tuple(c_lo)

    _, thr, _, _ = lax.while_loop(unresolved, refine, (jnp.int32(0), lo0, hi0, c0_lo))

    def emit(c0, carry):
        for rows, t in zip(halves, thr):
            x = s_ref[rows, pl.ds(c0, tc)]
            o_ref[rows, pl.ds(c0, tc)] = jnp.where(x >= t, 0.0, NEG_BIG).astype(o_ref.dtype)
        return carry

    over_chunks(emit, 0, 0, n_c)

    def fill(c0, carry):
        o_ref[:, pl.ds(c0, tc)] = jnp.full((n_half * SELECT_ROWS, tc), NEG_BIG, o_ref.dtype)
        return carry

    over_chunks(fill, 0, n_c, n_all)


def _select_kernel(s_ref, o_ref, *, n_half, sub, n_cols, k_top, rows_per_batch, n_groups):
    tc = sub * LANES
    n_all = n_cols // tc
    body = functools.partial(_select_body, s_ref, o_ref, n_half=n_half, sub=sub, n_all=n_all, k_top=k_top)
    if rows_per_batch:
        tr = n_half * SELECT_ROWS
        r0 = (pl.program_id(0) % (rows_per_batch // tr)) * tr
        rows_per_group = rows_per_batch // n_groups
        for g in range(n_groups):
            @pl.when(r0 // rows_per_group == g)
            def _(g=g):
                body(n_c=((g + 1) * rows_per_group - 1) // tc + 1, unroll=True)
    else:
        body(n_c=n_all, unroll=False)


def _select(scores, k_top, sub, rows_per_batch, out_dtype, n_half=1, n_groups=1):
    m, n_cols = scores.shape
    tr = n_half * SELECT_ROWS
    kern = functools.partial(_select_kernel, n_half=n_half, sub=sub, n_cols=n_cols, k_top=k_top,
                             rows_per_batch=rows_per_batch, n_groups=n_groups)
    return pl.pallas_call(
        kern,
        grid=(m // tr,),
        in_specs=[pl.BlockSpec((tr, n_cols), lambda i: (i, 0))],
        out_specs=pl.BlockSpec((tr, n_cols), lambda i: (i, 0)),
        out_shape=jax.ShapeDtypeStruct((m, n_cols), out_dtype),
        compiler_params=_cparams(1),
        name="select",
    )(scores)


def _attn_prompt_kernel(qi_ref, kj_ref, q_ref, k_ref, v_ref, b_ref, o_ref, m_ref, l_ref, acc_ref,
                        *, tq, ts):
    step = pl.program_id(1)
    i, j = qi_ref[step], kj_ref[step]
    n_rep = ts // LANES

    @pl.when(j == 0)
    def _():
        m_ref[...] = jnp.full(m_ref.shape, NEG_BIG, F32)
        l_ref[...] = jnp.zeros(l_ref.shape, F32)
        acc_ref[...] = jnp.zeros(acc_ref.shape, F32)

    def accumulate():
        bias = b_ref[...].astype(F32)
        nt = (((1,), (1,)), ((), ()))
        heads = [slice(h * DH_A, (h + 1) * DH_A) for h in range(H_A)]
        scores = [lax.dot_general(q_ref[:, sl], k_ref[:, sl], nt, preferred_element_type=F32) + bias
                  for sl in heads]
        probs, alphas = [], []
        for h, s in enumerate(scores):
            m_prev = m_ref[h]
            m_next = jnp.maximum(m_prev, jnp.max(s, axis=1, keepdims=True))
            alphas.append(jnp.exp2(m_prev - m_next))
            probs.append(jnp.exp2(s - jnp.tile(m_next, (1, n_rep))).astype(BF16))
            m_ref[h] = m_next
        for h, (sl, p, alpha) in enumerate(zip(heads, probs, alphas)):
            pv = jnp.dot(p, v_ref[:, 2 * h * DH_A:(2 * h + 2) * DH_A], preferred_element_type=F32)
            acc_ref[:, sl] = acc_ref[:, sl] * alpha + pv[:, :DH_A]
            l_ref[h] = l_ref[h] * alpha + pv[:, DH_A:]

    accumulate()

    @pl.when(j == (i * tq + tq - 1) // ts)
    def _():
        for h in range(H_A):
            sl = slice(h * DH_A, (h + 1) * DH_A)
            o_ref[:, sl] = (acc_ref[:, sl] / l_ref[h]).astype(o_ref.dtype)


def _attn_prompt(q_bf, k_bf, v_bf, bias, n_batch, t_len, tq=256, ts=512):
    nq, ns = t_len // tq, t_len // ts
    pairs = [(i, j) for i in range(nq) for j in range((i * tq + tq - 1) // ts + 1)]
    q_idx = jnp.asarray([p[0] for p in pairs], I32)
    k_idx = jnp.asarray([p[1] for p in pairs], I32)
    kern = functools.partial(_attn_prompt_kernel, tq=tq, ts=ts)
    return pl.pallas_call(
        kern,
        grid_spec=pltpu.PrefetchScalarGridSpec(
            num_scalar_prefetch=2,
            grid=(n_batch, len(pairs)),
            in_specs=[pl.BlockSpec((tq, WA), lambda b, s, qi, kj: (b * nq + qi[s], 0)),
                      pl.BlockSpec((ts, WA), lambda b, s, qi, kj: (b * ns + kj[s], 0)),
                      pl.BlockSpec((ts, 2 * WA), lambda b, s, qi, kj: (b * ns + kj[s], 0)),
                      pl.BlockSpec((tq, ts), lambda b, s, qi, kj: (b * nq + qi[s], kj[s]))],
            out_specs=pl.BlockSpec((tq, WA), lambda b, s, qi, kj: (b * nq + qi[s], 0)),
            scratch_shapes=[pltpu.VMEM((H_A, tq, LANES), F32), pltpu.VMEM((H_A, tq, LANES), F32),
                            pltpu.VMEM((tq, WA), F32)]),
        out_shape=jax.ShapeDtypeStruct((n_batch * t_len, WA), F32),
        compiler_params=_cparams(2),
        name="attn_prompt",
    )(q_idx, k_idx, q_bf, k_bf, v_bf, bias)


def _log_sigmoid(x):
    return jnp.minimum(x, 0.0) - jnp.log1p(jnp.exp(-jnp.abs(x)))


def _bf16_split3(x):
    x1 = x.astype(BF16)
    r1 = x - x1.astype(F32)
    x2 = r1.astype(BF16)
    x3 = (r1 - x2.astype(F32)).astype(BF16)
    return x1, x2, x3


def _gla_kernel(q_ref, k_ref, v_ref, tail_ref, wg_ref, bg_ref, s0_ref, o_ref, sfin_ref,
                st_ref, cum_all, kk_all, xs_all, *, tb, chunk, sub):
    c_id = pl.program_id(1)
    n_sub = chunk // sub
    nt = (((1,), (1,)), ((), ()))
    tn = (((0,), (0,)), ((), ()))

    @pl.when(c_id == 0)
    def _():
        for h in range(H_B):
            st_ref[h] = s0_ref[h].T

    rows = lax.broadcasted_iota(I32, (chunk, LANES), 0)
    cols = lax.broadcasted_iota(I32, (chunk, LANES), 1)
    tri = (lax.broadcasted_iota(I32, (chunk, chunk), 1)
           <= lax.broadcasted_iota(I32, (chunk, chunk), 0)).astype(BF16)
    ones = jnp.ones((LANES, LANES), BF16)

    def load(ref, r0, n_rows, c0, width):
        x = ref[pl.ds(r0, n_rows), c0:c0 + width]
        if n_rows < chunk:
            x = jnp.concatenate([x, jnp.zeros((chunk - n_rows, width), x.dtype)], axis=0)
        return x

    n_rows = min(tb, chunk)
    n_chunks = max(tb // chunk, 1)
    items = [(c, h) for c in range(n_chunks) for h in range(H_B)]
    slot_of = {it: i for i, it in enumerate(items)}
    lane_hit = [(cols % sub == sl) & (cols // sub == rows // sub) & (cols < chunk)
                for sl in range(sub)]
    row_blk = rows // sub

    ab = [load(tail_ref, c * chunk, n_rows, 0, N_TAIL)[:, L_AB:L_AB + GATE_RANK].astype(BF16)
          for c in range(n_chunks)]
    wg = [wg_ref[:, h * DK_B:(h + 1) * DK_B].astype(BF16) for h in range(H_B)]
    q, k, vb, la = {}, {}, {}, {}
    for c, h in items:
        q[c, h] = load(q_ref, c * chunk, n_rows, h * DK_B, DK_B) * (DK_B ** -0.5)
        k[c, h] = load(k_ref, c * chunk, n_rows, h * DK_B, DK_B)
        vb[c, h] = load(v_ref, c * chunk, n_rows, h * DV_B, DV_B).astype(BF16)
        x = jnp.dot(ab[c], wg[h], preferred_element_type=F32) + bg_ref[:, h * DK_B:(h + 1) * DK_B]
        la[c, h] = _log_sigmoid(x) / GATE_TAU
        if n_rows < chunk:
            la[c, h] = jnp.where(rows < n_rows, la[c, h], 0.0)

    cum = {}
    for it in items:
        l1, l2, l3 = _bf16_split3(la[it])
        cum[it] = (jnp.dot(tri, l1, preferred_element_type=F32)
                   + jnp.dot(tri, l2, preferred_element_type=F32)
                   + jnp.dot(tri, l3, preferred_element_type=F32))
        cum_all[slot_of[it]] = cum[it]
        kk_all[slot_of[it]] = k[it]

    att = {}
    for it in items:
        cum_ref = cum_all.at[slot_of[it]]
        a = jnp.zeros((chunk, LANES), F32)
        for i in range(1, n_sub):
            r_i = cum_ref[i * sub - 1:i * sub, :]
            qt = (q[it] * jnp.exp(jnp.minimum(cum[it] - r_i, 0.0))).astype(BF16)
            kt = (k[it] * jnp.exp(jnp.minimum(r_i - cum[it], 0.0))).astype(BF16)
            a_i = lax.dot_general(qt, kt, nt, preferred_element_type=F32)
            if chunk < LANES:
                a_i = jnp.concatenate([a_i, jnp.zeros((chunk, LANES - chunk), F32)], axis=1)
            a = jnp.where((row_blk == i) & (cols < i * sub), a_i, a)
        att[it] = a

    for it in items:
        cum_ref, kk_ref, xs_ref = (r.at[slot_of[it]] for r in (cum_all, kk_all, xs_all))
        for sl in range(sub):
            cum_s = jnp.concatenate(
                [jnp.broadcast_to(cum_ref[i * sub + sl:i * sub + sl + 1, :], (sub, LANES))
                 for i in range(n_sub)], axis=0)
            k_s = jnp.concatenate(
                [jnp.broadcast_to(kk_ref[i * sub + sl:i * sub + sl + 1, :], (sub, LANES))
                 for i in range(n_sub)], axis=0)
            dec = jnp.where(rows % sub >= sl, jnp.exp(jnp.minimum(cum[it] - cum_s, 0.0)), 0.0)
            xs_ref[sl * chunk:(sl + 1) * chunk, :] = (q[it] * k_s * dec).astype(BF16)
    n_x = sub * chunk
    red = jnp.dot(xs_all[...].reshape(len(items) * n_x, LANES), ones, preferred_element_type=F32)

    o_intra, upd, qdec, sdec = {}, {}, {}, {}
    for it in items:
        a = att[it]
        base = slot_of[it] * n_x
        for sl in range(sub):
            a = jnp.where(lane_hit[sl], red[base + sl * chunk:base + (sl + 1) * chunk, :], a)
        o_intra[it] = jnp.dot(a[:, :chunk].astype(BF16), vb[it], preferred_element_type=F32)
        last = cum_all[slot_of[it], chunk - 1:chunk, :]
        kd = (k[it] * jnp.exp(last - cum[it])).astype(BF16)
        upd[it] = lax.dot_general(vb[it], kd, tn, preferred_element_type=F32)
        qdec[it] = (q[it] * jnp.exp(cum[it])).astype(BF16)
        sdec[it] = jnp.exp(last)

    for h in range(H_B):
        st = st_ref[h]
        for c in range(n_chunks):
            o = o_intra[c, h] + lax.dot_general(qdec[c, h], st.astype(BF16), nt,
                                                preferred_element_type=F32)
            o_ref[c * chunk:c * chunk + n_rows, h * DV_B:(h + 1) * DV_B] = o[:n_rows]
            st = st * sdec[c, h] + upd[c, h]
        st_ref[h] = st

    @pl.when(c_id == pl.num_programs(1) - 1)
    def _():
        for h in range(H_B):
            sfin_ref[h] = st_ref[h].T


def _gla(z, tail, w_gate_up, b_gate, s0, n_batch, t_len, tb, chunk, sub):
    nb = t_len // tb
    n_slots = max(tb // chunk, 1) * H_B
    kern = functools.partial(_gla_kernel, tb=tb, chunk=chunk, sub=sub)
    return pl.pallas_call(
        kern,
        grid=(n_batch, nb),
        in_specs=[pl.BlockSpec((tb, WKB), lambda b, c: (b * nb + c, C_QB // WKB)),
                  pl.BlockSpec((tb, WKB), lambda b, c: (b * nb + c, C_KB // WKB)),
                  pl.BlockSpec((tb, WVB), lambda b, c: (b * nb + c, C_VB // WVB)),
                  pl.BlockSpec((tb, N_TAIL), lambda b, c: (b * nb + c, 0)),
                  pl.BlockSpec((GATE_RANK, WKB), lambda b, c: (0, 0)),
                  pl.BlockSpec((1, WKB), lambda b, c: (0, 0)),
                  pl.BlockSpec((None, H_B, DK_B, DV_B), lambda b, c: (b, 0, 0, 0))],
        out_specs=[pl.BlockSpec((tb, WVB), lambda b, c: (b * nb + c, 0)),
                   pl.BlockSpec((None, H_B, DK_B, DV_B), lambda b, c: (b, 0, 0, 0))],
        out_shape=[jax.ShapeDtypeStruct((n_batch * t_len, WVB), F32),
                   jax.ShapeDtypeStruct((n_batch, H_B, DK_B, DV_B), F32)],
        scratch_shapes=[pltpu.VMEM((H_B, DV_B, DK_B), F32), pltpu.VMEM((n_slots, chunk, LANES), F32),
                        pltpu.VMEM((n_slots, chunk, LANES), F32),
                        pltpu.VMEM((n_slots, sub * chunk, LANES), BF16)],
        compiler_params=_cparams(2),
        name="gla",
    )(z, z, z, tail, w_gate_up, b_gate, s0)


def _silu(x):
    return x / (1.0 + jnp.exp(-x))


def _merge_kernel(att_ref, ga_ref, ob_ref, gb_ref, gn_ref, wo_ref, x_ref, nf_ref, o_ref):
    a = (att_ref[...] * _silu(ga_ref[...])).astype(BF16)
    y = jnp.dot(a, wo_ref[:WA, :], preferred_element_type=F32)
    gn = gn_ref[...]
    for h in range(H_B):
        sl = slice(h * DV_B, (h + 1) * DV_B)
        ob = ob_ref[:, sl]
        bn = ob * lax.rsqrt(jnp.mean(ob * ob, axis=-1, keepdims=True) + EPS) * gn
        bp = (bn * _silu(gb_ref[:, sl])).astype(BF16)
        y = y + jnp.dot(bp, wo_ref[WA + h * DV_B:WA + (h + 1) * DV_B, :], preferred_element_type=F32)
    xo = x_ref[...] + y
    o_ref[...] = xo * lax.rsqrt(jnp.mean(xo * xo, axis=-1, keepdims=True) + EPS) * nf_ref[...]


def _merge(att, z, ob, gla_norm, w_out_bf, x2d, norm_f, tm):
    m = x2d.shape[0]
    return pl.pallas_call(
        _merge_kernel,
        grid=(m // tm,),
        in_specs=[pl.BlockSpec((tm, WA), lambda i: (i, 0)),
                  pl.BlockSpec((tm, WA), lambda i: (i, C_GA // WA)),
                  pl.BlockSpec((tm, WVB), lambda i: (i, 0)),
                  pl.BlockSpec((tm, WVB), lambda i: (i, C_GB // WVB)),
                  pl.BlockSpec((1, DV_B), lambda i: (0, 0)),
                  pl.BlockSpec((WA + WVB, D_MODEL), lambda i: (0, 0)),
                  pl.BlockSpec((tm, D_MODEL), lambda i: (i, 0)),
                  pl.BlockSpec((1, D_MODEL), lambda i: (0, 0))],
        out_specs=pl.BlockSpec((tm, D_MODEL), lambda i: (i, 0)),
        out_shape=jax.ShapeDtypeStruct((m, D_MODEL), F32),
        compiler_params=_cparams(1),
        name="merge",
    )(att, z, ob, z, gla_norm, w_out_bf, x2d, norm_f)


PAGES_PER_STEP = 32


def _idx_pages_kernel(pt_ref, q_ref, w_ref, *refs, n_pages, t_dec, causal):
    page_refs, o_ref = refs[:n_pages], refs[n_pages]
    q = q_ref[...]
    w = w_ref[...]
    for p in range(n_pages):
        kp = page_refs[p][...].astype(BF16)
        d = jnp.dot(q, kp, preferred_element_type=F32)
        r = w * jnp.maximum(d, 0.0)
        acc = r[0:t_dec]
        for h in range(1, H_IDX):
            acc = acc + r[h * t_dec:(h + 1) * t_dec]
        if causal:
            t_i = lax.broadcasted_iota(I32, acc.shape, 0)
            s_i = lax.broadcasted_iota(I32, acc.shape, 1)
            acc = jnp.where(s_i <= t_i, acc, -jnp.inf)
        o_ref[:, p * PAGE_SIZE:(p + 1) * PAGE_SIZE] = acc


def _idx_sample_past(page_table, q_stk, w_stk, pool_ik, n_seq, t_dec, n_pages):
    pps = min(PAGES_PER_STEP, n_pages)
    n_g = n_pages // pps
    rows = H_IDX * t_dec
    kern = functools.partial(_idx_pages_kernel, n_pages=pps, t_dec=t_dec, causal=False)
    page_spec = lambda r: pl.BlockSpec(
        (None, D_IDX, PAGE_SIZE), lambda b, g, pt, r=r: (pt[b * n_pages + g * pps + r], 0, 0))
    return pl.pallas_call(
        kern,
        grid_spec=pltpu.PrefetchScalarGridSpec(
            num_scalar_prefetch=1,
            grid=(n_seq, n_g),
            in_specs=[pl.BlockSpec((rows, D_IDX), lambda b, g, pt: (b, 0)),
                      pl.BlockSpec((rows, 1), lambda b, g, pt: (b, 0))]
                     + [page_spec(r) for r in range(pps)],
            out_specs=pl.BlockSpec((t_dec, pps * PAGE_SIZE), lambda b, g, pt: (b, g))),
        out_shape=jax.ShapeDtypeStruct((n_seq * t_dec, n_pages * PAGE_SIZE), F32),
        compiler_params=_cparams(2),
        name="idx_sample_past",
    )(page_table.reshape(-1), q_stk, w_stk, *([pool_ik] * pps))


def _idx_sample_new(q_stk, w_stk, ki_new_pad, n_seq, t_dec):
    rows = H_IDX * t_dec
    kern = functools.partial(_idx_pages_kernel, None, n_pages=1, t_dec=t_dec, causal=True)
    return pl.pallas_call(
        kern,
        grid=(n_seq,),
        in_specs=[pl.BlockSpec((rows, D_IDX), lambda b: (b, 0)),
                  pl.BlockSpec((rows, 1), lambda b: (b, 0)),
                  pl.BlockSpec((None, D_IDX, PAGE_SIZE), lambda b: (b, 0, 0))],
        out_specs=pl.BlockSpec((t_dec, PAGE_SIZE), lambda b: (b, 0)),
        out_shape=jax.ShapeDtypeStruct((n_seq * t_dec, PAGE_SIZE), F32),
        compiler_params=_cparams(1),
        name="idx_sample_new",
    )(q_stk, w_stk, ki_new_pad)


ATTN_PAGES_PER_STEP = 8


def _attn_sample_kernel(pt_ref, q_ref, sel_ref, selnew_ref, e_ref, knew_ref, vnew_ref, *refs,
                        n_pages, t_dec):
    k_refs, v_refs = refs[:n_pages], refs[n_pages:2 * n_pages]
    o_ref, m_ref, l_ref, acc_ref = refs[2 * n_pages:]
    _attn_sample_body(pl.program_id(1), pl.num_programs(1), q_ref, sel_ref, selnew_ref, e_ref,
                      knew_ref, vnew_ref, k_refs, v_refs, o_ref, m_ref, l_ref, acc_ref, t_dec)


def _attn_sample_body(g, n_g, q_ref, sel_ref, selnew_ref, e_ref, knew_ref, vnew_ref, k_refs, v_refs,
                      o_ref, m_ref, l_ref, acc_ref, t_dec):
    n_pages = len(k_refs)
    rows = H_A * t_dec
    nt = (((1,), (1,)), ((), ()))
    q = q_ref[...]
    n_col = PAGE_SIZE * H_A
    head_hot = (lax.broadcasted_iota(I32, (rows, LANES), 0) // t_dec
                == lax.broadcasted_iota(I32, (rows, LANES), 1)).astype(BF16)

    def update(sels, k_pages, v_pages):
        lhs = jnp.concatenate(
            [jnp.concatenate([jnp.tile(sel, (H_A, 1)).astype(BF16), head_hot], axis=1) for sel in sels],
            axis=0)
        bias = jnp.dot(lhs, e_ref[...], preferred_element_type=F32)
        scores = []
        m_next = m_ref[...]
        for i, k_page in enumerate(k_pages):
            s = lax.dot_general(q, k_page[...].astype(BF16), nt, preferred_element_type=F32)
            s = s + bias[i * rows:(i + 1) * rows]
            m_next = jnp.maximum(m_next, jnp.max(s, axis=1, keepdims=True))
            scores.append(s)
        alpha = jnp.exp2(m_ref[...] - m_next)
        m_rep = jnp.tile(m_next, (1, n_col // LANES))
        l_new = alpha * l_ref[...]
        acc = acc_ref[...] * alpha
        for s, v_page in zip(scores, v_pages):
            p = jnp.exp2(s - m_rep)
            l_new = l_new + jnp.sum(p, axis=1, keepdims=True)
            acc = acc + jnp.dot(p.astype(BF16), v_page[...].astype(BF16), preferred_element_type=F32)
        m_ref[...] = m_next
        l_ref[...] = l_new
        acc_ref[...] = acc

    @pl.when(g == 0)
    def _():
        m_ref[...] = jnp.full(m_ref.shape, NEG_BIG, F32)
        l_ref[...] = jnp.zeros(l_ref.shape, F32)
        acc_ref[...] = jnp.zeros(acc_ref.shape, F32)
        update([selnew_ref[...]], [knew_ref], [vnew_ref])

    update([sel_ref[:, p_i * PAGE_SIZE:(p_i + 1) * PAGE_SIZE] for p_i in range(n_pages)],
           k_refs, v_refs)

    @pl.when(g == n_g - 1)
    def _():
        o_ref[...] = acc_ref[...] / l_ref[...]


def _proj_attn_kernel(pt_ref, x_ref, g_ref, wa_ref, wb_ref, q_ref, sel_ref, selnew_ref, e_ref,
                      knew_ref, vnew_ref, *refs, n_pages, t_dec, n_a, n_j, n_g):
    k_refs, v_refs = refs[:n_pages], refs[n_pages:2 * n_pages]
    z_ref, att_ref, h_ref, m_ref, l_ref, acc_ref = refs[2 * n_pages:]
    step = pl.program_id(0)
    _proj_body(step % n_j, x_ref, g_ref, wa_ref, wb_ref, z_ref, h_ref, n_a)
    _attn_sample_body(step % n_g, n_g, q_ref, sel_ref, selnew_ref, e_ref, knew_ref, vnew_ref,
                      k_refs, v_refs, att_ref, m_ref, l_ref, acc_ref, t_dec)


def _proj_attn(x2d, g, w_a, n_cols_a, w_b, tm, tn,
               page_table, q_stk, sel_past, sel_new, expand, knew_pad, vnew_pad, pool_k, pool_v,
               n_seq, t_dec, n_pages):
    m, k = x2d.shape
    n_a, n_b = n_cols_a // tn, w_b.shape[1] // tn
    n_j = n_a + n_b
    pps = ATTN_PAGES_PER_STEP
    n_g = n_pages // pps
    n_steps = (m // tm) * n_j
    assert n_steps == n_seq * n_g
    rows = H_A * t_dec
    n_col = PAGE_SIZE * H_A
    kern = functools.partial(_proj_attn_kernel, n_pages=pps, t_dec=t_dec, n_a=n_a, n_j=n_j, n_g=n_g)
    page_spec = lambda r: pl.BlockSpec(
        (None, n_col, DH_A), lambda s, pt, r=r: (pt[(s // n_g) * n_pages + (s % n_g) * pps + r], 0, 0))
    return pl.pallas_call(
        kern,
        grid_spec=pltpu.PrefetchScalarGridSpec(
            num_scalar_prefetch=1,
            grid=(n_steps,),
            in_specs=[pl.BlockSpec((tm, k), lambda s, pt: (s // n_j, 0)),
                      pl.BlockSpec((1, k), lambda s, pt: (0, 0)),
                      pl.BlockSpec((k, tn), lambda s, pt: (0, jnp.minimum(s % n_j, n_a - 1))),
                      pl.BlockSpec((k, tn), lambda s, pt: (0, jnp.maximum(s % n_j - n_a, 0))),
                      pl.BlockSpec((rows, DH_A), lambda s, pt: (s // n_g, 0)),
                      pl.BlockSpec((t_dec, pps * PAGE_SIZE), lambda s, pt: (s // n_g, s % n_g)),
                      pl.BlockSpec((t_dec, PAGE_SIZE), lambda s, pt: (s // n_g, 0)),
                      pl.BlockSpec((2 * PAGE_SIZE, n_col), lambda s, pt: (0, 0)),
                      pl.BlockSpec((None, n_col, DH_A), lambda s, pt: (s // n_g, 0, 0)),
                      pl.BlockSpec((None, n_col, DH_A), lambda s, pt: (s // n_g, 0, 0))]
                     + [page_spec(r) for r in range(pps)] * 2,
            out_specs=[pl.BlockSpec((tm, tn), lambda s, pt: (s // n_j, s % n_j)),
                       pl.BlockSpec((rows, DH_A), lambda s, pt: (s // n_g, 0))],
            scratch_shapes=[pltpu.VMEM((tm, k), BF16),
                            pltpu.VMEM((rows, LANES), F32), pltpu.VMEM((rows, LANES), F32),
                            pltpu.VMEM((rows, DH_A), F32)]),
        out_shape=[jax.ShapeDtypeStruct((m, n_j * tn), F32),
                   jax.ShapeDtypeStruct((n_seq * rows, DH_A), F32)],
        compiler_params=pltpu.CompilerParams(dimension_semantics=("arbitrary",),
                                             vmem_limit_bytes=FUSED_VMEM_LIMIT),
        name="proj_attn",
    )(page_table.reshape(-1), x2d, g, w_a, w_b, q_stk, sel_past, sel_new, expand, knew_pad, vnew_pad,
      *([pool_k] * pps), *([pool_v] * pps))


def _attn_sample(page_table, q_stk, sel_past, sel_new, expand, knew_pad, vnew_pad, pool_k, pool_v,
                 n_seq, t_dec, n_pages):
    pps = ATTN_PAGES_PER_STEP
    n_g = n_pages // pps
    rows = H_A * t_dec
    n_col = PAGE_SIZE * H_A
    kern = functools.partial(_attn_sample_kernel, n_pages=pps, t_dec=t_dec)
    page_spec = lambda r: pl.BlockSpec(
        (None, n_col, DH_A), lambda b, g, pt, r=r: (pt[b * n_pages + g * pps + r], 0, 0))
    return pl.pallas_call(
        kern,
        grid_spec=pltpu.PrefetchScalarGridSpec(
            num_scalar_prefetch=1,
            grid=(n_seq, n_g),
            in_specs=[pl.BlockSpec((rows, DH_A), lambda b, g, pt: (b, 0)),
                      pl.BlockSpec((t_dec, pps * PAGE_SIZE), lambda b, g, pt: (b, g)),
                      pl.BlockSpec((t_dec, PAGE_SIZE), lambda b, g, pt: (b, 0)),
                      pl.BlockSpec((2 * PAGE_SIZE, n_col), lambda b, g, pt: (0, 0)),
                      pl.BlockSpec((None, n_col, DH_A), lambda b, g, pt: (b, 0, 0)),
                      pl.BlockSpec((None, n_col, DH_A), lambda b, g, pt: (b, 0, 0))]
                     + [page_spec(r) for r in range(pps)] * 2,
            out_specs=pl.BlockSpec((rows, DH_A), lambda b, g, pt: (b, 0)),
            scratch_shapes=[pltpu.VMEM((rows, LANES), F32), pltpu.VMEM((rows, LANES), F32),
                            pltpu.VMEM((rows, DH_A), F32)]),
        out_shape=jax.ShapeDtypeStruct((n_seq * rows, DH_A), F32),
        compiler_params=_cparams(2),
        name="attn_sample",
    )(page_table.reshape(-1), q_stk, sel_past, sel_new, expand, knew_pad, vnew_pad,
      *([pool_k] * pps), *([pool_v] * pps))


def _group_common(x2d, norm_in, w_parts, pos, tm, tn, tr, z=None):
    w_full, w_gla, w_tail = w_parts
    if z is None:
        z = _proj(x2d, norm_in, w_full, C_QB, w_gla, tm, tn)
    tail = _proj(x2d, norm_in, w_tail, N_TAIL, None, tm, N_TAIL)
    tab = _rope_tables(pos)
    return z, tail, _rope(z, tail, tab, tr)


def kernel(x_prompt, x_sample, cache_k, cache_v, cache_idx_k, state_gla, page_table,
           norm_in, w_in, w_gate_up, b_gate, gla_norm, w_out, norm_f):
    n_p, t_p, _ = x_prompt.shape
    n_s, t_s, _ = x_sample.shape
    n_pages = page_table.shape[1]
    past = n_pages * PAGE_SIZE
    assert cache_k.shape[0] == 1, "single layer"

    w_parts = _prep_w_in(w_in[0])
    w_out_bf = w_out[0].astype(BF16)
    g_in, g_gla, g_f = norm_in[0][None], gla_norm[0][None], norm_f[None]
    wg, bg = w_gate_up[0], b_gate[0][None]

    m_s = n_s * t_s
    pos_s = jnp.tile(past + jnp.arange(t_s), n_s)
    xs2d = x_sample.reshape(m_s, D_MODEL)
    zs, tails, (newk_s, newv_s, newik_s, q_s, k_s, v_s, qi_s, kie_s, kio_s) = _group_common(
        xs2d, g_in, w_parts, pos_s, m_s, 1024, m_s)
    qi_stk = qi_s.reshape(n_s, t_s, H_IDX, D_IDX).transpose(0, 2, 1, 3).reshape(n_s * H_IDX * t_s, D_IDX)
    w_stk = (tails[:, L_WI:L_WI + H_IDX] * IDX_W_SCALE).reshape(n_s, t_s, H_IDX).transpose(0, 2, 1)
    w_stk = w_stk.reshape(n_s * H_IDX * t_s, 1)
    q_stk = q_s.reshape(n_s, t_s, H_A, DH_A).transpose(0, 2, 1, 3).reshape(n_s * H_A * t_s, DH_A)
    pad_rows = lambda a, n: jnp.pad(a, ((0, 0), (0, n - a.shape[1]), (0, 0)))
    ki_new_pad = jnp.pad(newik_s.reshape(n_s, t_s, D_IDX).transpose(0, 2, 1),
                         ((0, 0), (0, 0), (0, PAGE_SIZE - t_s)))
    knew_pad = pad_rows(newk_s.reshape(n_s, t_s * H_A, DH_A), PAGE_SIZE * H_A)
    vnew_pad = pad_rows(newv_s.reshape(n_s, t_s * H_A, DH_A), PAGE_SIZE * H_A)
    pool_ik = jnp.swapaxes(cache_idx_k[0], 1, 2)
    pool_k = cache_k[0].reshape(-1, PAGE_SIZE * H_A, DH_A)
    pool_v = cache_v[0].reshape(-1, PAGE_SIZE * H_A, DH_A)

    sc_past = _idx_sample_past(page_table, qi_stk, w_stk, pool_ik, n_s, t_s, n_pages)
    sc_new = _idx_sample_new(qi_stk, w_stk, ki_new_pad, n_s, t_s)
    sc_s = jnp.concatenate([sc_past, sc_new], axis=1)
    k_top_s = min(TOPK_MAX, (past + t_s) // 4)
    n_tiles_s = sc_s.shape[1] // LANES
    sub_s = max(d for d in range(1, 6) if n_tiles_s % d == 0)
    bias_s = _select(sc_s, k_top_s, sub_s, 0, F32)
    col = jnp.arange(PAGE_SIZE * H_A)[None, :]
    row = jnp.arange(2 * PAGE_SIZE)[:, None]
    expand = jnp.where(row < PAGE_SIZE, (col // H_A == row).astype(F32),
                       jnp.where((row < PAGE_SIZE + H_A) & (col % H_A != row - PAGE_SIZE), NEG_BIG, 0.0)
                       ).astype(BF16)
    attn_args = (page_table, q_stk, bias_s[:, :past], bias_s[:, past:], expand,
                 knew_pad, vnew_pad, pool_k, pool_v, n_s, t_s, n_pages)

    xp2d = x_prompt.reshape(n_p * t_p, D_MODEL)
    tm_p, tn_p = 512, 512
    fused_steps = (xp2d.shape[0] // tm_p) * (N_MAIN // tn_p)
    if fused_steps == n_s * (n_pages // ATTN_PAGES_PER_STEP):
        zp, att_stk = _proj_attn(xp2d, g_in, w_parts[0], C_QB, w_parts[1], tm_p, tn_p, *attn_args)
    else:
        zp, att_stk = None, _attn_sample(*attn_args)
    att_s = att_stk.reshape(n_s, H_A, t_s, DH_A).transpose(0, 2, 1, 3).reshape(m_s, WA)

    zp, tailp, (newk_p, newv_p, newik_p, q_p, k_p, v_p, qi_p, kie_p, kio_p) = _group_common(
        xp2d, g_in, w_parts, jnp.arange(t_p), 1024, 512, 256, z=zp)
    sc_p = _idx_prompt(qi_p, tailp, kie_p, kio_p, n_p, t_p)
    n_groups_p = max(1, min(4, t_p // 1024))
    bias_p = _select(sc_p, min(TOPK_MAX, t_p // 4), 4, t_p, BF16, n_half=2, n_groups=n_groups_p)
    att_p = _attn_prompt(q_p, k_p, v_p, bias_p, n_p, t_p)
    s0_p = jnp.zeros((n_p, H_B, DK_B, DV_B), F32)
    ob_p, sfin_p = _gla(zp, tailp, wg, bg, s0_p, n_p, t_p, 256, 64, 16)
    y_p = _merge(att_p, zp, ob_p, g_gla, w_out_bf, xp2d, g_f, 256)

    ob_s, sfin_s = _gla(zs, tails, wg, bg, state_gla[0], n_s, t_s, t_s, 16, 16)
    y_s = _merge(att_s, zs, ob_s, g_gla, w_out_bf, xs2d, g_f, m_s)

    return (y_p.reshape(n_p, t_p, D_MODEL), y_s.reshape(n_s, t_s, D_MODEL),
            newk_p.reshape(1, n_p, t_p, H_A, DH_A), newv_p.reshape(1, n_p, t_p, H_A, DH_A),
            newik_p.reshape(1, n_p, t_p, D_IDX), sfin_p[None],
            newk_s.reshape(1, n_s, t_s, H_A, DH_A), newv_s.reshape(1, n_s, t_s, H_A, DH_A),
            newik_s.reshape(1, n_s, t_s, D_IDX), sfin_s[None])
```

```python
import functools
import math

import jax
import jax.numpy as jnp
import numpy as np
from jax import lax
from jax.experimental import pallas as pl
from jax.experimental.pallas import tpu as pltpu

F32 = jnp.float32
BF16 = jnp.bfloat16
I32 = jnp.int32

D_MODEL = 2048
PAGE_SIZE = 128
DH_A = 128
H_A = 8
H_IDX = 16
D_IDX = 64
TOPK_MAX = 256
DK_B = 128
DV_B = 256
H_B = 4
GATE_RANK = 16
GATE_TAU = 16.0
ROPE_THETA = 10000.0
EPS = 1e-6
NEG_BIG = -1e30
WA = H_A * DH_A
WI = H_IDX * D_IDX
WKB = H_B * DK_B
WVB = H_B * DV_B

C_QA, C_KA, C_VA, C_GA, C_QI = 0, WA, 2 * WA, 3 * WA, 4 * WA
C_QB = C_QI + WI
C_KB = C_QB + WKB
C_VB = C_KB + WKB
C_GB = C_VB + WVB
N_MAIN = C_GB + WVB
L_KI, L_WI, L_AB = 0, D_IDX, D_IDX + H_IDX
N_TAIL = 128

LANES = 128
INT_MIN = -2 ** 31
VMEM_LIMIT = 48 * 1024 * 1024
FUSED_VMEM_LIMIT = 56 * 1024 * 1024


def _prep_w_in(w_in):
    o_ki = C_QI + WI
    o_qb = o_ki + D_IDX + H_IDX
    o_ab = o_qb + 2 * WKB + 2 * WVB
    w_bf = w_in.astype(BF16)
    pad = jnp.zeros((w_in.shape[0], N_TAIL - D_IDX - H_IDX - GATE_RANK), BF16)
    tail = jnp.concatenate([w_bf[:, o_ki:o_qb], w_bf[:, o_ab:], pad], axis=1)
    return w_bf, w_bf[:, o_qb:o_ab], tail


def _cparams(n_axes):
    return pltpu.CompilerParams(dimension_semantics=("arbitrary",) * n_axes,
                                vmem_limit_bytes=VMEM_LIMIT)


def _proj_kernel(x_ref, g_ref, wa_ref, wb_ref, o_ref, h_ref, *, n_a):
    _proj_body(pl.program_id(1), x_ref, g_ref, wa_ref, wb_ref, o_ref, h_ref, n_a)


def _proj_body(j, x_ref, g_ref, wa_ref, wb_ref, o_ref, h_ref, n_a):
    @pl.when(j == 0)
    def _():
        x = x_ref[...]
        ms = jnp.mean(x * x, axis=-1, keepdims=True)
        h_ref[...] = (x * lax.rsqrt(ms + EPS) * g_ref[...]).astype(BF16)

    w = jnp.where(j < n_a, wa_ref[...], wb_ref[...])
    o_ref[...] = jnp.dot(h_ref[...], w, preferred_element_type=F32)


def _proj(x2d, g, w_a, n_cols_a, w_b, tm, tn):
    m, k = x2d.shape
    n_a = n_cols_a // tn
    if w_b is None:
        w_b, n_b = w_a, 0
    else:
        n_b = w_b.shape[1] // tn
    return pl.pallas_call(
        functools.partial(_proj_kernel, n_a=n_a),
        grid=(m // tm, n_a + n_b),
        in_specs=[pl.BlockSpec((tm, k), lambda i, j: (i, 0)),
                  pl.BlockSpec((1, k), lambda i, j: (0, 0)),
                  pl.BlockSpec((k, tn), lambda i, j: (0, jnp.minimum(j, n_a - 1))),
                  pl.BlockSpec((k, tn), lambda i, j: (0, jnp.maximum(j - n_a, 0)))],
        out_specs=pl.BlockSpec((tm, tn), lambda i, j: (i, j)),
        out_shape=jax.ShapeDtypeStruct((m, (n_a + n_b) * tn), F32),
        scratch_shapes=[pltpu.VMEM((tm, k), BF16)],
        compiler_params=_cparams(2),
        name="proj",
    )(x2d, g, w_a, w_b)


def _rope128(x, cos, sin_signed):
    return x * cos + pltpu.roll(x, 64, axis=1) * sin_signed


def _rope64(x, cos, sin_lo, sin_hi):
    return x * cos + pltpu.roll(x, 96, axis=1) * sin_lo + pltpu.roll(x, 32, axis=1) * sin_hi


def _rope_kernel(qa_ref, ka_ref, va_ref, qi_ref, tail_ref, tab_ref,
                 newk_ref, newv_ref, newik_ref, qbf_ref, kbf_ref, vbf_ref, qibf_ref,
                 kie_ref, kio_ref):
    cos_a, sin_a = tab_ref[0], tab_ref[1]
    cos_i, sin_lo, sin_hi = tab_ref[2], tab_ref[3], tab_ref[4]
    scale = (DH_A ** -0.5) * math.log2(math.e)
    for h in range(H_A):
        sl = slice(h * DH_A, (h + 1) * DH_A)
        q = _rope128(qa_ref[:, sl], cos_a, sin_a)
        qbf_ref[:, sl] = (q * scale).astype(BF16)
        k = _rope128(ka_ref[:, sl], cos_a, sin_a)
        newk_ref[:, sl] = k
        kbf_ref[:, sl] = k.astype(BF16)
    v = va_ref[...]
    newv_ref[...] = v
    ones = jnp.ones((v.shape[0], DH_A), BF16)
    for h in range(H_A):
        vbf_ref[:, 2 * h * DH_A:(2 * h + 1) * DH_A] = v[:, h * DH_A:(h + 1) * DH_A].astype(BF16)
        vbf_ref[:, (2 * h + 1) * DH_A:(2 * h + 2) * DH_A] = ones
    for j in range(WI // LANES):
        sl = slice(j * LANES, (j + 1) * LANES)
        qibf_ref[:, sl] = _rope64(qi_ref[:, sl], cos_i, sin_lo, sin_hi).astype(BF16)
    ki = _rope64(tail_ref[...], cos_i, sin_lo, sin_hi)
    newik_ref[...] = ki[:, L_KI:L_KI + D_IDX]
    lane = lax.broadcasted_iota(I32, ki.shape, 1)
    ki_lo = jnp.where(lane < D_IDX, ki, 0.0)
    kie_ref[...] = ki_lo.astype(BF16)
    kio_ref[...] = pltpu.roll(ki_lo, D_IDX, axis=1).astype(BF16)


def _rope(z, tail, tab, tm):
    m = z.shape[0]
    n_tab = tab.shape[1] // tm
    col = lambda c: pl.BlockSpec((tm, WA), lambda i, c=c: (i, c))
    row = lambda w: pl.BlockSpec((tm, w), lambda i: (i, 0))
    return pl.pallas_call(
        _rope_kernel,
        grid=(m // tm,),
        in_specs=[col(C_QA // WA), col(C_KA // WA), col(C_VA // WA), col(C_QI // WA),
                  row(N_TAIL),
                  pl.BlockSpec((5, tm, LANES), lambda i: (0, i % n_tab, 0))],
        out_specs=[row(WA), row(WA), row(D_IDX), row(WA), row(WA), row(2 * WA), row(WI),
                   row(LANES), row(LANES)],
        out_shape=[jax.ShapeDtypeStruct((m, WA), F32), jax.ShapeDtypeStruct((m, WA), F32),
                   jax.ShapeDtypeStruct((m, D_IDX), F32),
                   jax.ShapeDtypeStruct((m, WA), BF16), jax.ShapeDtypeStruct((m, WA), BF16),
                   jax.ShapeDtypeStruct((m, 2 * WA), BF16), jax.ShapeDtypeStruct((m, WI), BF16),
                   jax.ShapeDtypeStruct((m, LANES), BF16), jax.ShapeDtypeStruct((m, LANES), BF16)],
        compiler_params=_cparams(1),
        name="rope",
    )(z, z, z, z, tail, tab)


def _rope_tables(pos):
    pos = np.asarray(pos).astype(np.float32)[:, None]
    half_a, half_i = DH_A // 2, D_IDX // 2
    theta = np.float32(ROPE_THETA)
    inv_a = theta ** (-np.arange(half_a, dtype=np.float32) / np.float32(half_a))
    inv_i = theta ** (-np.arange(half_i, dtype=np.float32) / np.float32(half_i))
    ang_a, ang_i = (pos * inv_a).astype(np.float32), (pos * inv_i).astype(np.float32)
    ca, sa = np.cos(ang_a), np.sin(ang_a)
    ci, si = np.cos(ang_i), np.sin(ang_i)
    zi = np.zeros_like(si)
    return jnp.asarray(np.stack([
        np.concatenate([ca, ca], axis=1),
        np.concatenate([-sa, sa], axis=1),
        np.concatenate([ci, ci, ci, ci], axis=1),
        np.concatenate([-si, zi, -si, zi], axis=1),
        np.concatenate([zi, si, zi, si], axis=1)]).astype(np.float32))


IDX_W_SCALE = (H_IDX ** -0.5) * (D_IDX ** -0.5)


def _idx_scores(q_ref, w, kie, kio):
    nt = (((1,), (1,)), ((), ()))
    acc = jnp.zeros((q_ref.shape[0], kie.shape[0]), F32)
    for j in range(H_IDX // 2):
        qp = q_ref[:, j * LANES:(j + 1) * LANES]
        de = lax.dot_general(qp, kie, nt, preferred_element_type=F32)
        acc = acc + w[:, 2 * j:2 * j + 1] * jnp.maximum(de, 0.0)
        do = lax.dot_general(qp, kio, nt, preferred_element_type=F32)
        acc = acc + w[:, 2 * j + 1:2 * j + 2] * jnp.maximum(do, 0.0)
    return acc


def _idx_prompt_kernel(q_ref, tail_ref, kie_ref, kio_ref, o_ref, *, tq, tc, t_len):
    i = pl.program_id(1)
    w = tail_ref[:, L_WI:L_WI + H_IDX] * IDX_W_SCALE
    n_c = (i * tq + tq - 1) // tc + 1
    row = i * tq + lax.broadcasted_iota(I32, (tq, tc), 0)

    def body(c, carry):
        c0 = pl.multiple_of(c * tc, tc)
        acc = _idx_scores(q_ref, w, kie_ref[pl.ds(c0, tc), :], kio_ref[pl.ds(c0, tc), :])
        col = c0 + lax.broadcasted_iota(I32, (tq, tc), 1)
        o_ref[:, pl.ds(c0, tc)] = jnp.where(col <= row, acc, -jnp.inf)
        return carry

    lax.fori_loop(0, n_c, body, 0)

    def fill(c, carry):
        c0 = pl.multiple_of(c * tc, tc)
        o_ref[:, pl.ds(c0, tc)] = jnp.full((tq, tc), -jnp.inf, F32)
        return carry

    lax.fori_loop(n_c, t_len // tc, fill, 0)


def _idx_prompt(qi_bf, tail, kie, kio, n_batch, t_len, tq=128, tc=256):
    nq = t_len // tq
    kern = functools.partial(_idx_prompt_kernel, tq=tq, tc=tc, t_len=t_len)
    return pl.pallas_call(
        kern,
        grid=(n_batch, nq),
        in_specs=[pl.BlockSpec((tq, WI), lambda b, i: (b * nq + i, 0)),
                  pl.BlockSpec((tq, N_TAIL), lambda b, i: (b * nq + i, 0)),
                  pl.BlockSpec((t_len, LANES), lambda b, i: (b, 0)),
                  pl.BlockSpec((t_len, LANES), lambda b, i: (b, 0))],
        out_specs=pl.BlockSpec((tq, t_len), lambda b, i: (b * nq + i, 0)),
        out_shape=jax.ShapeDtypeStruct((n_batch * t_len, t_len), F32),
        compiler_params=_cparams(2),
        name="idx_prompt",
    )(qi_bf, tail, kie, kio)


SELECT_STEPS_PER_CHECK = 3
SELECT_MAX_CHECKS = 16
SELECT_ROWS = 128


def _select_body(s_ref, o_ref, *, n_half, sub, n_all, n_c, unroll, k_top):
    tc = sub * LANES
    kf = float(k_top)
    halves = [pl.ds(hh * SELECT_ROWS, SELECT_ROWS) for hh in range(n_half)]
    shape = (SELECT_ROWS, LANES)

    def over_chunks(fn, init, lo_c, hi_c):
        if unroll:
            carry = init
            for c in range(lo_c, hi_c):
                carry = fn(c * tc, carry)
            return carry
        return lax.fori_loop(lo_c, hi_c, lambda c, carry: fn(pl.multiple_of(c * tc, tc), carry), init)

    def stats(c0, carry):
        out = []
        for rows, (mn, mx, cnt) in zip(halves, carry):
            for u in range(sub):
                x = s_ref[rows, pl.ds(c0 + u * LANES, LANES)]
                valid = x > -jnp.inf
                mn = jnp.minimum(mn, jnp.where(valid, x, jnp.inf))
                mx = jnp.maximum(mx, x)
                cnt = cnt + jnp.where(valid, 1.0, 0.0)
            out.append((mn, mx, cnt))
        return tuple(out)

    st = over_chunks(stats, tuple((jnp.full(shape, jnp.inf, F32), jnp.full(shape, -jnp.inf, F32),
                                   jnp.zeros(shape, F32)) for _ in halves), 0, n_c)
    lo0 = tuple(jnp.min(mn, axis=1, keepdims=True) for mn, _, _ in st)
    hi0 = tuple(jnp.max(mx, axis=1, keepdims=True) for _, mx, _ in st)
    c0_lo = tuple(jnp.sum(cnt, axis=1, keepdims=True) for _, _, cnt in st)

    def count_ge(thrs):
        thr_b = [jnp.broadcast_to(t, shape) for t in thrs]

        def body(c0, accs):
            out = []
            for rows, tb, acc in zip(halves, thr_b, accs):
                for u in range(sub):
                    x = s_ref[rows, pl.ds(c0 + u * LANES, LANES)]
                    acc = acc + jnp.where(x >= tb, 1.0, 0.0)
                out.append(acc)
            return tuple(out)

        accs = over_chunks(body, tuple(jnp.zeros(shape, F32) for _ in halves), 0, n_c)
        return [jnp.sum(a, axis=1, keepdims=True) for a in accs]

    def unresolved(state):
        it, _, _, c_lo = state
        worst = c_lo[0]
        for c in c_lo[1:]:
            worst = jnp.maximum(worst, c)
        return jnp.logical_and(it < SELECT_MAX_CHECKS, jnp.max(worst) > kf)

    def refine(state):
        it, lo, hi, c_lo = state
        lo, hi, c_lo = list(lo), list(hi), list(c_lo)
        for _ in range(SELECT_STEPS_PER_CHECK):
            mid = [0.5 * a + 0.5 * b for a, b in zip(lo, hi)]
            cnt = count_ge(mid)
            for i in range(n_half):
                ok = cnt[i] >= kf
                lo[i] = jnp.where(ok, mid[i], lo[i])
                c_lo[i] = jnp.where(ok, cnt[i], c_lo[i])
                hi[i] = jnp.where(ok, hi[i], mid[i])
        return it + 1, tuple(lo), tuple(hi), tuple(c_lo)

    _, thr, _, _ = lax.while_loop(unresolved, refine, (jnp.int32(0), lo0, hi0, c0_lo))

    def emit(c0, carry):
        for rows, t in zip(halves, thr):
            x = s_ref[rows, pl.ds(c0, tc)]
            o_ref[rows, pl.ds(c0, tc)] = jnp.where(x >= t, 0.0, NEG_BIG).astype(o_ref.dtype)
        return carry

    over_chunks(emit, 0, 0, n_c)

    def fill(c0, carry):
        o_ref[:, pl.ds(c0, tc)] = jnp.full((n_half * SELECT_ROWS, tc), NEG_BIG, o_ref.dtype)
        return carry

    over_chunks(fill, 0, n_c, n_all)


def _select_kernel(s_ref, o_ref, *, n_half, sub, n_cols, k_top, rows_per_batch, n_groups):
    tc = sub * LANES
    n_all = n_cols // tc
    body = functools.partial(_select_body, s_ref, o_ref, n_half=n_half, sub=sub, n_all=n_all, k_top=k_top)
    if rows_per_batch:
        tr = n_half * SELECT_ROWS
        r0 = (pl.program_id(0) % (rows_per_batch // tr)) * tr
        rows_per_group = rows_per_batch // n_groups
        for g in range(n_groups):
            @pl.when(r0 // rows_per_group == g)
            def _(g=g):
                body(n_c=((g + 1) * rows_per_group - 1) // tc + 1, unroll=True)
    else:
        body(n_c=n_all, unroll=False)


def _select(scores, k_top, sub, rows_per_batch, out_dtype, n_half=1, n_groups=1):
    m, n_cols = scores.shape
    tr = n_half * SELECT_ROWS
    kern = functools.partial(_select_kernel, n_half=n_half, sub=sub, n_cols=n_cols, k_top=k_top,
                             rows_per_batch=rows_per_batch, n_groups=n_groups)
    return pl.pallas_call(
        kern,
        grid=(m // tr,),
        in_specs=[pl.BlockSpec((tr, n_cols), lambda i: (i, 0))],
        out_specs=pl.BlockSpec((tr, n_cols), lambda i: (i, 0)),
        out_shape=jax.ShapeDtypeStruct((m, n_cols), out_dtype),
        compiler_params=_cparams(1),
        name="select",
    )(scores)


def _attn_prompt_kernel(qi_ref, kj_ref, q_ref, k_ref, v_ref, b_ref, o_ref, m_ref, l_ref, acc_ref,
                        *, tq, ts):
    step = pl.program_id(1)
    i, j = qi_ref[step], kj_ref[step]
    n_rep = ts // LANES

    @pl.when(j == 0)
    def _():
        m_ref[...] = jnp.full(m_ref.shape, NEG_BIG, F32)
        l_ref[...] = jnp.zeros(l_ref.shape, F32)
        acc_ref[...] = jnp.zeros(acc_ref.shape, F32)

    def accumulate():
        bias = b_ref[...].astype(F32)
        nt = (((1,), (1,)), ((), ()))
        heads = [slice(h * DH_A, (h + 1) * DH_A) for h in range(H_A)]
        scores = [lax.dot_general(q_ref[:, sl], k_ref[:, sl], nt, preferred_element_type=F32) + bias
                  for sl in heads]
        probs, alphas = [], []
        for h, s in enumerate(scores):
            m_prev = m_ref[h]
            m_next = jnp.maximum(m_prev, jnp.max(s, axis=1, keepdims=True))
            alphas.append(jnp.exp2(m_prev - m_next))
            probs.append(jnp.exp2(s - jnp.tile(m_next, (1, n_rep))).astype(BF16))
            m_ref[h] = m_next
        for h, (sl, p, alpha) in enumerate(zip(heads, probs, alphas)):
            pv = jnp.dot(p, v_ref[:, 2 * h * DH_A:(2 * h + 2) * DH_A], preferred_element_type=F32)
            acc_ref[:, sl] = acc_ref[:, sl] * alpha + pv[:, :DH_A]
            l_ref[h] = l_ref[h] * alpha + pv[:, DH_A:]

    accumulate()

    @pl.when(j == (i * tq + tq - 1) // ts)
    def _():
        for h in range(H_A):
            sl = slice(h * DH_A, (h + 1) * DH_A)
            o_ref[:, sl] = (acc_ref[:, sl] / l_ref[h]).astype(o_ref.dtype)


def _attn_prompt(q_bf, k_bf, v_bf, bias, n_batch, t_len, tq=256, ts=1024):
    nq, ns = t_len // tq, t_len // ts
    pairs = [(i, j) for i in range(nq) for j in range((i * tq + tq - 1) // ts + 1)]
    q_idx = jnp.asarray([p[0] for p in pairs], I32)
    k_idx = jnp.asarray([p[1] for p in pairs], I32)
    kern = functools.partial(_attn_prompt_kernel, tq=tq, ts=ts)
    return pl.pallas_call(
        kern,
        grid_spec=pltpu.PrefetchScalarGridSpec(
            num_scalar_prefetch=2,
            grid=(n_batch, len(pairs)),
            in_specs=[pl.BlockSpec((tq, WA), lambda b, s, qi, kj: (b * nq + qi[s], 0)),
                      pl.BlockSpec((ts, WA), lambda b, s, qi, kj: (b * ns + kj[s], 0)),
                      pl.BlockSpec((ts, 2 * WA), lambda b, s, qi, kj: (b * ns + kj[s], 0)),
                      pl.BlockSpec((tq, ts), lambda b, s, qi, kj: (b * nq + qi[s], kj[s]))],
            out_specs=pl.BlockSpec((tq, WA), lambda b, s, qi, kj: (b * nq + qi[s], 0)),
            scratch_shapes=[pltpu.VMEM((H_A, tq, LANES), F32), pltpu.VMEM((H_A, tq, LANES), F32),
                            pltpu.VMEM((tq, WA), F32)]),
        out_shape=jax.ShapeDtypeStruct((n_batch * t_len, WA), F32),
        compiler_params=_cparams(2),
        name="attn_prompt",
    )(q_idx, k_idx, q_bf, k_bf, v_bf, bias)


def _log_sigmoid(x):
    return jnp.minimum(x, 0.0) - jnp.log1p(jnp.exp(-jnp.abs(x)))


def _bf16_split3(x):
    x1 = x.astype(BF16)
    r1 = x - x1.astype(F32)
    x2 = r1.astype(BF16)
    x3 = (r1 - x2.astype(F32)).astype(BF16)
    return x1, x2, x3


def _gla_kernel(q_ref, k_ref, v_ref, tail_ref, wg_ref, bg_ref, s0_ref, o_ref, sfin_ref,
                st_ref, cum_all, kk_all, xs_all, *, tb, chunk, sub):
    c_id = pl.program_id(1)
    n_sub = chunk // sub
    nt = (((1,), (1,)), ((), ()))
    tn = (((0,), (0,)), ((), ()))

    @pl.when(c_id == 0)
    def _():
        for h in range(H_B):
            st_ref[h] = s0_ref[h].T

    rows = lax.broadcasted_iota(I32, (chunk, LANES), 0)
    cols = lax.broadcasted_iota(I32, (chunk, LANES), 1)
    tri = (lax.broadcasted_iota(I32, (chunk, chunk), 1)
           <= lax.broadcasted_iota(I32, (chunk, chunk), 0)).astype(BF16)
    ones = jnp.ones((LANES, LANES), BF16)

    def load(ref, r0, n_rows, c0, width):
        x = ref[pl.ds(r0, n_rows), c0:c0 + width]
        if n_rows < chunk:
            x = jnp.concatenate([x, jnp.zeros((chunk - n_rows, width), x.dtype)], axis=0)
        return x

    n_rows = min(tb, chunk)
    n_chunks = max(tb // chunk, 1)
    items = [(c, h) for c in range(n_chunks) for h in range(H_B)]
    slot_of = {it: i for i, it in enumerate(items)}
    lane_hit = [(cols % sub == sl) & (cols // sub == rows // sub) & (cols < chunk)
                for sl in range(sub)]
    row_blk = rows // sub

    ab = [load(tail_ref, c * chunk, n_rows, 0, N_TAIL)[:, L_AB:L_AB + GATE_RANK].astype(BF16)
          for c in range(n_chunks)]
    wg = [wg_ref[:, h * DK_B:(h + 1) * DK_B].astype(BF16) for h in range(H_B)]
    q, k, vb, la = {}, {}, {}, {}
    for c, h in items:
        q[c, h] = load(q_ref, c * chunk, n_rows, h * DK_B, DK_B) * (DK_B ** -0.5)
        k[c, h] = load(k_ref, c * chunk, n_rows, h * DK_B, DK_B)
        vb[c, h] = load(v_ref, c * chunk, n_rows, h * DV_B, DV_B).astype(BF16)
        x = jnp.dot(ab[c], wg[h], preferred_element_type=F32) + bg_ref[:, h * DK_B:(h + 1) * DK_B]
        la[c, h] = _log_sigmoid(x) / GATE_TAU
        if n_rows < chunk:
            la[c, h] = jnp.where(rows < n_rows, la[c, h], 0.0)

    cum = {}
    for it in items:
        l1, l2, l3 = _bf16_split3(la[it])
        cum[it] = (jnp.dot(tri, l1, preferred_element_type=F32)
                   + jnp.dot(tri, l2, preferred_element_type=F32)
                   + jnp.dot(tri, l3, preferred_element_type=F32))
        cum_all[slot_of[it]] = cum[it]
        kk_all[slot_of[it]] = k[it]

    att = {}
    for it in items:
        cum_ref = cum_all.at[slot_of[it]]
        a = jnp.zeros((chunk, LANES), F32)
        for i in range(1, n_sub):
            r_i = cum_ref[i * sub - 1:i * sub, :]
            qt = (q[it] * jnp.exp(jnp.minimum(cum[it] - r_i, 0.0))).astype(BF16)
            kt = (k[it] * jnp.exp(jnp.minimum(r_i - cum[it], 0.0))).astype(BF16)
            a_i = lax.dot_general(qt, kt, nt, preferred_element_type=F32)
            if chunk < LANES:
                a_i = jnp.concatenate([a_i, jnp.zeros((chunk, LANES - chunk), F32)], axis=1)
            a = jnp.where((row_blk == i) & (cols < i * sub), a_i, a)
        att[it] = a

    for it in items:
        cum_ref, kk_ref, xs_ref = (r.at[slot_of[it]] for r in (cum_all, kk_all, xs_all))
        for sl in range(sub):
            cum_s = jnp.concatenate(
                [jnp.broadcast_to(cum_ref[i * sub + sl:i * sub + sl + 1, :], (sub, LANES))
                 for i in range(n_sub)], axis=0)
            k_s = jnp.concatenate(
                [jnp.broadcast_to(kk_ref[i * sub + sl:i * sub + sl + 1, :], (sub, LANES))
                 for i in range(n_sub)], axis=0)
            dec = jnp.where(rows % sub >= sl, jnp.exp(jnp.minimum(cum[it] - cum_s, 0.0)), 0.0)
            xs_ref[sl * chunk:(sl + 1) * chunk, :] = (q[it] * k_s * dec).astype(BF16)
    n_x = sub * chunk
    red = jnp.dot(xs_all[...].reshape(len(items) * n_x, LANES), ones, preferred_element_type=F32)

    o_intra, upd, qdec, sdec = {}, {}, {}, {}
    for it in items:
        a = att[it]
        base = slot_of[it] * n_x
        for sl in range(sub):
            a = jnp.where(lane_hit[sl], red[base + sl * chunk:base + (sl + 1) * chunk, :], a)
        o_intra[it] = jnp.dot(a[:, :chunk].astype(BF16), vb[it], preferred_element_type=F32)
        last = cum_all[slot_of[it], chunk - 1:chunk, :]
        kd = (k[it] * jnp.exp(last - cum[it])).astype(BF16)
        upd[it] = lax.dot_general(vb[it], kd, tn, preferred_element_type=F32)
        qdec[it] = (q[it] * jnp.exp(cum[it])).astype(BF16)
        sdec[it] = jnp.exp(last)

    for h in range(H_B):
        st = st_ref[h]
        for c in range(n_chunks):
            o = o_intra[c, h] + lax.dot_general(qdec[c, h], st.astype(BF16), nt,
                                                preferred_element_type=F32)
            o_ref[c * chunk:c * chunk + n_rows, h * DV_B:(h + 1) * DV_B] = o[:n_rows]
            st = st * sdec[c, h] + upd[c, h]
        st_ref[h] = st

    @pl.when(c_id == pl.num_programs(1) - 1)
    def _():
        for h in range(H_B):
            sfin_ref[h] = st_ref[h].T


def _gla(z, tail, w_gate_up, b_gate, s0, n_batch, t_len, tb, chunk, sub):
    nb = t_len // tb
    n_slots = max(tb // chunk, 1) * H_B
    kern = functools.partial(_gla_kernel, tb=tb, chunk=chunk, sub=sub)
    return pl.pallas_call(
        kern,
        grid=(n_batch, nb),
        in_specs=[pl.BlockSpec((tb, WKB), lambda b, c: (b * nb + c, C_QB // WKB)),
                  pl.BlockSpec((tb, WKB), lambda b, c: (b * nb + c, C_KB // WKB)),
                  pl.BlockSpec((tb, WVB), lambda b, c: (b * nb + c, C_VB // WVB)),
                  pl.BlockSpec((tb, N_TAIL), lambda b, c: (b * nb + c, 0)),
                  pl.BlockSpec((GATE_RANK, WKB), lambda b, c: (0, 0)),
                  pl.BlockSpec((1, WKB), lambda b, c: (0, 0)),
                  pl.BlockSpec((None, H_B, DK_B, DV_B), lambda b, c: (b, 0, 0, 0))],
        out_specs=[pl.BlockSpec((tb, WVB), lambda b, c: (b * nb + c, 0)),
                   pl.BlockSpec((None, H_B, DK_B, DV_B), lambda b, c: (b, 0, 0, 0))],
        out_shape=[jax.ShapeDtypeStruct((n_batch * t_len, WVB), F32),
                   jax.ShapeDtypeStruct((n_batch, H_B, DK_B, DV_B), F32)],
        scratch_shapes=[pltpu.VMEM((H_B, DV_B, DK_B), F32), pltpu.VMEM((n_slots, chunk, LANES), F32),
                        pltpu.VMEM((n_slots, chunk, LANES), F32),
                        pltpu.VMEM((n_slots, sub * chunk, LANES), BF16)],
        compiler_params=_cparams(2),
        name="gla",
    )(z, z, z, tail, w_gate_up, b_gate, s0)


def _silu(x):
    return x / (1.0 + jnp.exp(-x))


def _merge_kernel(att_ref, ga_ref, ob_ref, gb_ref, gn_ref, wo_ref, x_ref, nf_ref, o_ref):
    a = (att_ref[...] * _silu(ga_ref[...])).astype(BF16)
    y = jnp.dot(a, wo_ref[:WA, :], preferred_element_type=F32)
    gn = gn_ref[...]
    for h in range(H_B):
        sl = slice(h * DV_B, (h + 1) * DV_B)
        ob = ob_ref[:, sl]
        bn = ob * lax.rsqrt(jnp.mean(ob * ob, axis=-1, keepdims=True) + EPS) * gn
        bp = (bn * _silu(gb_ref[:, sl])).astype(BF16)
        y = y + jnp.dot(bp, wo_ref[WA + h * DV_B:WA + (h + 1) * DV_B, :], preferred_element_type=F32)
    xo = x_ref[...] + y
    o_ref[...] = xo * lax.rsqrt(jnp.mean(xo * xo, axis=-1, keepdims=True) + EPS) * nf_ref[...]


def _merge(att, z, ob, gla_norm, w_out_bf, x2d, norm_f, tm):
    m = x2d.shape[0]
    return pl.pallas_call(
        _merge_kernel,
        grid=(m // tm,),
        in_specs=[pl.BlockSpec((tm, WA), lambda i: (i, 0)),
                  pl.BlockSpec((tm, WA), lambda i: (i, C_GA // WA)),
                  pl.BlockSpec((tm, WVB), lambda i: (i, 0)),
                  pl.BlockSpec((tm, WVB), lambda i: (i, C_GB // WVB)),
                  pl.BlockSpec((1, DV_B), lambda i: (0, 0)),
                  pl.BlockSpec((WA + WVB, D_MODEL), lambda i: (0, 0)),
                  pl.BlockSpec((tm, D_MODEL), lambda i: (i, 0)),
                  pl.BlockSpec((1, D_MODEL), lambda i: (0, 0))],
        out_specs=pl.BlockSpec((tm, D_MODEL), lambda i: (i, 0)),
        out_shape=jax.ShapeDtypeStruct((m, D_MODEL), F32),
        compiler_params=_cparams(1),
        name="merge",
    )(att, z, ob, z, gla_norm, w_out_bf, x2d, norm_f)


PAGES_PER_STEP = 32


def _idx_pages_kernel(pt_ref, q_ref, w_ref, *refs, n_pages, t_dec, causal):
    page_refs, o_ref = refs[:n_pages], refs[n_pages]
    q = q_ref[...]
    w = w_ref[...]
    for p in range(n_pages):
        kp = page_refs[p][...].astype(BF16)
        d = jnp.dot(q, kp, preferred_element_type=F32)
        r = w * jnp.maximum(d, 0.0)
        acc = r[0:t_dec]
        for h in range(1, H_IDX):
            acc = acc + r[h * t_dec:(h + 1) * t_dec]
        if causal:
            t_i = lax.broadcasted_iota(I32, acc.shape, 0)
            s_i = lax.broadcasted_iota(I32, acc.shape, 1)
            acc = jnp.where(s_i <= t_i, acc, -jnp.inf)
        o_ref[:, p * PAGE_SIZE:(p + 1) * PAGE_SIZE] = acc


def _idx_sample_past(page_table, q_stk, w_stk, pool_ik, n_seq, t_dec, n_pages):
    pps = min(PAGES_PER_STEP, n_pages)
    n_g = n_pages // pps
    rows = H_IDX * t_dec
    kern = functools.partial(_idx_pages_kernel, n_pages=pps, t_dec=t_dec, causal=False)
    page_spec = lambda r: pl.BlockSpec(
        (None, D_IDX, PAGE_SIZE), lambda b, g, pt, r=r: (pt[b * n_pages + g * pps + r], 0, 0))
    return pl.pallas_call(
        kern,
        grid_spec=pltpu.PrefetchScalarGridSpec(
            num_scalar_prefetch=1,
            grid=(n_seq, n_g),
            in_specs=[pl.BlockSpec((rows, D_IDX), lambda b, g, pt: (b, 0)),
                      pl.BlockSpec((rows, 1), lambda b, g, pt: (b, 0))]
                     + [page_spec(r) for r in range(pps)],
            out_specs=pl.BlockSpec((t_dec, pps * PAGE_SIZE), lambda b, g, pt: (b, g))),
        out_shape=jax.ShapeDtypeStruct((n_seq * t_dec, n_pages * PAGE_SIZE), F32),
        compiler_params=_cparams(2),
        name="idx_sample_past",
    )(page_table.reshape(-1), q_stk, w_stk, *([pool_ik] * pps))


def _idx_sample_new(q_stk, w_stk, ki_new_pad, n_seq, t_dec):
    rows = H_IDX * t_dec
    kern = functools.partial(_idx_pages_kernel, None, n_pages=1, t_dec=t_dec, causal=True)
    return pl.pallas_call(
        kern,
        grid=(n_seq,),
        in_specs=[pl.BlockSpec((rows, D_IDX), lambda b: (b, 0)),
                  pl.BlockSpec((rows, 1), lambda b: (b, 0)),
                  pl.BlockSpec((None, D_IDX, PAGE_SIZE), lambda b: (b, 0, 0))],
        out_specs=pl.BlockSpec((t_dec, PAGE_SIZE), lambda b: (b, 0)),
        out_shape=jax.ShapeDtypeStruct((n_seq * t_dec, PAGE_SIZE), F32),
        compiler_params=_cparams(1),
        name="idx_sample_new",
    )(q_stk, w_stk, ki_new_pad)


ATTN_PAGES_PER_STEP = 8


def _attn_sample_kernel(pt_ref, q_ref, sel_ref, selnew_ref, e_ref, knew_ref, vnew_ref, *refs,
                        n_pages, t_dec):
    k_refs, v_refs = refs[:n_pages], refs[n_pages:2 * n_pages]
    o_ref, m_ref, l_ref, acc_ref = refs[2 * n_pages:]
    _attn_sample_body(pl.program_id(1), pl.num_programs(1), q_ref, sel_ref, selnew_ref, e_ref,
                      knew_ref, vnew_ref, k_refs, v_refs, o_ref, m_ref, l_ref, acc_ref, t_dec)


def _attn_sample_body(g, n_g, q_ref, sel_ref, selnew_ref, e_ref, knew_ref, vnew_ref, k_refs, v_refs,
                      o_ref, m_ref, l_ref, acc_ref, t_dec):
    n_pages = len(k_refs)
    rows = H_A * t_dec
    nt = (((1,), (1,)), ((), ()))
    q = q_ref[...]
    n_col = PAGE_SIZE * H_A
    head_hot = (lax.broadcasted_iota(I32, (rows, LANES), 0) // t_dec
                == lax.broadcasted_iota(I32, (rows, LANES), 1)).astype(BF16)

    def update(sels, k_pages, v_pages):
        lhs = jnp.concatenate(
            [jnp.concatenate([jnp.tile(sel, (H_A, 1)).astype(BF16), head_hot], axis=1) for sel in sels],
            axis=0)
        bias = jnp.dot(lhs, e_ref[...], preferred_element_type=F32)
        scores = []
        m_next = m_ref[...]
        for i, k_page in enumerate(k_pages):
            s = lax.dot_general(q, k_page[...].astype(BF16), nt, preferred_element_type=F32)
            s = s + bias[i * rows:(i + 1) * rows]
            m_next = jnp.maximum(m_next, jnp.max(s, axis=1, keepdims=True))
            scores.append(s)
        alpha = jnp.exp2(m_ref[...] - m_next)
        m_rep = jnp.tile(m_next, (1, n_col // LANES))
        l_new = alpha * l_ref[...]
        acc = acc_ref[...] * alpha
        for s, v_page in zip(scores, v_pages):
            p = jnp.exp2(s - m_rep)
            l_new = l_new + jnp.sum(p, axis=1, keepdims=True)
            acc = acc + jnp.dot(p.astype(BF16), v_page[...].astype(BF16), preferred_element_type=F32)
        m_ref[...] = m_next
        l_ref[...] = l_new
        acc_ref[...] = acc

    @pl.when(g == 0)
    def _():
        m_ref[...] = jnp.full(m_ref.shape, NEG_BIG, F32)
        l_ref[...] = jnp.zeros(l_ref.shape, F32)
        acc_ref[...] = jnp.zeros(acc_ref.shape, F32)
        update([selnew_ref[...]], [knew_ref], [vnew_ref])

    update([sel_ref[:, p_i * PAGE_SIZE:(p_i + 1) * PAGE_SIZE] for p_i in range(n_pages)],
           k_refs, v_refs)

    @pl.when(g == n_g - 1)
    def _():
        o_ref[...] = acc_ref[...] / l_ref[...]


def _proj_attn_kernel(pt_ref, x_ref, g_ref, wa_ref, wb_ref, q_ref, sel_ref, selnew_ref, e_ref,
                      knew_ref, vnew_ref, *refs, n_pages, t_dec, n_a, n_j, n_g):
    k_refs, v_refs = refs[:n_pages], refs[n_pages:2 * n_pages]
    z_ref, att_ref, h_ref, m_ref, l_ref, acc_ref = refs[2 * n_pages:]
    step = pl.program_id(0)
    _proj_body(step % n_j, x_ref, g_ref, wa_ref, wb_ref, z_ref, h_ref, n_a)
    _attn_sample_body(step % n_g, n_g, q_ref, sel_ref, selnew_ref, e_ref, knew_ref, vnew_ref,
                      k_refs, v_refs, att_ref, m_ref, l_ref, acc_ref, t_dec)


def _proj_attn(x2d, g, w_a, n_cols_a, w_b, tm, tn,
               page_table, q_stk, sel_past, sel_new, expand, knew_pad, vnew_pad, pool_k, pool_v,
               n_seq, t_dec, n_pages):
    m, k = x2d.shape
    n_a, n_b = n_cols_a // tn, w_b.shape[1] // tn
    n_j = n_a + n_b
    pps = ATTN_PAGES_PER_STEP
    n_g = n_pages // pps
    n_steps = (m // tm) * n_j
    assert n_steps == n_seq * n_g
    rows = H_A * t_dec
    n_col = PAGE_SIZE * H_A
    kern = functools.partial(_proj_attn_kernel, n_pages=pps, t_dec=t_dec, n_a=n_a, n_j=n_j, n_g=n_g)
    page_spec = lambda r: pl.BlockSpec(
        (None, n_col, DH_A), lambda s, pt, r=r: (pt[(s // n_g) * n_pages + (s % n_g) * pps + r], 0, 0))
    return pl.pallas_call(
        kern,
        grid_spec=pltpu.PrefetchScalarGridSpec(
            num_scalar_prefetch=1,
            grid=(n_steps,),
            in_specs=[pl.BlockSpec((tm, k), lambda s, pt: (s // n_j, 0)),
                      pl.BlockSpec((1, k), lambda s, pt: (0, 0)),
                      pl.BlockSpec((k, tn), lambda s, pt: (0, jnp.minimum(s % n_j, n_a - 1))),
                      pl.BlockSpec((k, tn), lambda s, pt: (0, jnp.maximum(s % n_j - n_a, 0))),
                      pl.BlockSpec((rows, DH_A), lambda s, pt: (s // n_g, 0)),
                      pl.BlockSpec((t_dec, pps * PAGE_SIZE), lambda s, pt: (s // n_g, s % n_g)),
                      pl.BlockSpec((t_dec, PAGE_SIZE), lambda s, pt: (s // n_g, 0)),
                      pl.BlockSpec((2 * PAGE_SIZE, n_col), lambda s, pt: (0, 0)),
                      pl.BlockSpec((None, n_col, DH_A), lambda s, pt: (s // n_g, 0, 0)),
                      pl.BlockSpec((None, n_col, DH_A), lambda s, pt: (s // n_g, 0, 0))]
                     + [page_spec(r) for r in range(pps)] * 2,
            out_specs=[pl.BlockSpec((tm, tn), lambda s, pt: (s // n_j, s % n_j)),
                       pl.BlockSpec((rows, DH_A), lambda s, pt: (s // n_g, 0))],
            scratch_shapes=[pltpu.VMEM((tm, k), BF16),
                            pltpu.VMEM((rows, LANES), F32), pltpu.VMEM((rows, LANES), F32),
                            pltpu.VMEM((rows, DH_A), F32)]),
        out_shape=[jax.ShapeDtypeStruct((m, n_j * tn), F32),
                   jax.ShapeDtypeStruct((n_seq * rows, DH_A), F32)],
        compiler_params=pltpu.CompilerParams(dimension_semantics=("arbitrary",),
                                             vmem_limit_bytes=FUSED_VMEM_LIMIT),
        name="proj_attn",
    )(page_table.reshape(-1), x2d, g, w_a, w_b, q_stk, sel_past, sel_new, expand, knew_pad, vnew_pad,
      *([pool_k] * pps), *([pool_v] * pps))


def _attn_sample(page_table, q_stk, sel_past, sel_new, expand, knew_pad, vnew_pad, pool_k, pool_v,
                 n_seq, t_dec, n_pages):
    pps = ATTN_PAGES_PER_STEP
    n_g = n_pages // pps
    rows = H_A * t_dec
    n_col = PAGE_SIZE * H_A
    kern = functools.partial(_attn_sample_kernel, n_pages=pps, t_dec=t_dec)
    page_spec = lambda r: pl.BlockSpec(
        (None, n_col, DH_A), lambda b, g, pt, r=r: (pt[b * n_pages + g * pps + r], 0, 0))
    return pl.pallas_call(
        kern,
        grid_spec=pltpu.PrefetchScalarGridSpec(
            num_scalar_prefetch=1,
            grid=(n_seq, n_g),
            in_specs=[pl.BlockSpec((rows, DH_A), lambda b, g, pt: (b, 0)),
                      pl.BlockSpec((t_dec, pps * PAGE_SIZE), lambda b, g, pt: (b, g)),
                      pl.BlockSpec((t_dec, PAGE_SIZE), lambda b, g, pt: (b, 0)),
                      pl.BlockSpec((2 * PAGE_SIZE, n_col), lambda b, g, pt: (0, 0)),
                      pl.BlockSpec((None, n_col, DH_A), lambda b, g, pt: (b, 0, 0)),
                      pl.BlockSpec((None, n_col, DH_A), lambda b, g, pt: (b, 0, 0))]
                     + [page_spec(r) for r in range(pps)] * 2,
            out_specs=pl.BlockSpec((rows, DH_A), lambda b, g, pt: (b, 0)),
            scratch_shapes=[pltpu.VMEM((rows, LANES), F32), pltpu.VMEM((rows, LANES), F32),
                            pltpu.VMEM((rows, DH_A), F32)]),
        out_shape=jax.ShapeDtypeStruct((n_seq * rows, DH_A), F32),
        compiler_params=_cparams(2),
        name="attn_sample",
    )(page_table.reshape(-1), q_stk, sel_past, sel_new, expand, knew_pad, vnew_pad,
      *([pool_k] * pps), *([pool_v] * pps))


def _group_common(x2d, norm_in, w_parts, pos, tm, tn, tr, z=None):
    w_full, w_gla, w_tail = w_parts
    if z is None:
        z = _proj(x2d, norm_in, w_full, C_QB, w_gla, tm, tn)
    tail = _proj(x2d, norm_in, w_tail, N_TAIL, None, tm, N_TAIL)
    tab = _rope_tables(pos)
    return z, tail, _rope(z, tail, tab, tr)


def kernel(x_prompt, x_sample, cache_k, cache_v, cache_idx_k, state_gla, page_table,
           norm_in, w_in, w_gate_up, b_gate, gla_norm, w_out, norm_f):
    n_p, t_p, _ = x_prompt.shape
    n_s, t_s, _ = x_sample.shape
    n_pages = page_table.shape[1]
    past = n_pages * PAGE_SIZE
    assert cache_k.shape[0] == 1, "single layer"

    w_parts = _prep_w_in(w_in[0])
    w_out_bf = w_out[0].astype(BF16)
    g_in, g_gla, g_f = norm_in[0][None], gla_norm[0][None], norm_f[None]
    wg, bg = w_gate_up[0], b_gate[0][None]

    m_s = n_s * t_s
    pos_s = np.tile(past + np.arange(t_s), n_s)
    xs2d = x_sample.reshape(m_s, D_MODEL)
    zs, tails, (newk_s, newv_s, newik_s, q_s, k_s, v_s, qi_s, kie_s, kio_s) = _group_common(
        xs2d, g_in, w_parts, pos_s, m_s, 1024, m_s)
    qi_stk = qi_s.reshape(n_s, t_s, H_IDX, D_IDX).transpose(0, 2, 1, 3).reshape(n_s * H_IDX * t_s, D_IDX)
    w_stk = (tails[:, L_WI:L_WI + H_IDX] * IDX_W_SCALE).reshape(n_s, t_s, H_IDX).transpose(0, 2, 1)
    w_stk = w_stk.reshape(n_s * H_IDX * t_s, 1)
    q_stk = q_s.reshape(n_s, t_s, H_A, DH_A).transpose(0, 2, 1, 3).reshape(n_s * H_A * t_s, DH_A)
    pad_rows = lambda a, n: jnp.pad(a, ((0, 0), (0, n - a.shape[1]), (0, 0)))
    ki_new_pad = jnp.pad(newik_s.reshape(n_s, t_s, D_IDX).transpose(0, 2, 1),
                         ((0, 0), (0, 0), (0, PAGE_SIZE - t_s)))
    knew_pad = pad_rows(newk_s.reshape(n_s, t_s * H_A, DH_A), PAGE_SIZE * H_A)
    vnew_pad = pad_rows(newv_s.reshape(n_s, t_s * H_A, DH_A), PAGE_SIZE * H_A)
    pool_ik = jnp.swapaxes(cache_idx_k[0], 1, 2)
    pool_k = cache_k[0].reshape(-1, PAGE_SIZE * H_A, DH_A)
    pool_v = cache_v[0].reshape(-1, PAGE_SIZE * H_A, DH_A)

    sc_past = _idx_sample_past(page_table, qi_stk, w_stk, pool_ik, n_s, t_s, n_pages)
    sc_new = _idx_sample_new(qi_stk, w_stk, ki_new_pad, n_s, t_s)
    sc_s = jnp.concatenate([sc_past, sc_new], axis=1)
    k_top_s = min(TOPK_MAX, (past + t_s) // 4)
    n_tiles_s = sc_s.shape[1] // LANES
    sub_s = max(d for d in range(1, 6) if n_tiles_s % d == 0)
    bias_s = _select(sc_s, k_top_s, sub_s, 0, F32)
    col = jnp.arange(PAGE_SIZE * H_A)[None, :]
    row = jnp.arange(2 * PAGE_SIZE)[:, None]
    expand = jnp.where(row < PAGE_SIZE, (col // H_A == row).astype(F32),
                       jnp.where((row < PAGE_SIZE + H_A) & (col % H_A != row - PAGE_SIZE), NEG_BIG, 0.0)
                       ).astype(BF16)
    attn_args = (page_table, q_stk, bias_s[:, :past], bias_s[:, past:], expand,
                 knew_pad, vnew_pad, pool_k, pool_v, n_s, t_s, n_pages)

    xp2d = x_prompt.reshape(n_p * t_p, D_MODEL)
    tm_p, tn_p = 512, 512
    fused_steps = (xp2d.shape[0] // tm_p) * (N_MAIN // tn_p)
    if fused_steps == n_s * (n_pages // ATTN_PAGES_PER_STEP):
        zp, att_stk = _proj_attn(xp2d, g_in, w_parts[0], C_QB, w_parts[1], tm_p, tn_p, *attn_args)
    else:
        zp, att_stk = None, _attn_sample(*attn_args)
    att_s = att_stk.reshape(n_s, H_A, t_s, DH_A).transpose(0, 2, 1, 3).reshape(m_s, WA)

    zp, tailp, (newk_p, newv_p, newik_p, q_p, k_p, v_p, qi_p, kie_p, kio_p) = _group_common(
        xp2d, g_in, w_parts, np.arange(t_p), 1024, 512, 256, z=zp)
    sc_p = _idx_prompt(qi_p, tailp, kie_p, kio_p, n_p, t_p)
    n_groups_p = max(1, min(4, t_p // 1024))
    bias_p = _select(sc_p, min(TOPK_MAX, t_p // 4), 4, t_p, BF16, n_half=2, n_groups=n_groups_p)
    att_p = _attn_prompt(q_p, k_p, v_p, bias_p, n_p, t_p)
    s0_p = jnp.zeros((n_p, H_B, DK_B, DV_B), F32)
    ob_p, sfin_p = _gla(zp, tailp, wg, bg, s0_p, n_p, t_p, 256, 64, 16)
    y_p = _merge(att_p, zp, ob_p, g_gla, w_out_bf, xp2d, g_f, 256)

    ob_s, sfin_s = _gla(zs, tails, wg, bg, state_gla[0], n_s, t_s, t_s, 16, 16)
    y_s = _merge(att_s, zs, ob_s, g_gla, w_out_bf, xs2d, g_f, m_s)

    return (y_p.reshape(n_p, t_p, D_MODEL), y_s.reshape(n_s, t_s, D_MODEL),
            newk_p.reshape(1, n_p, t_p, H_A, DH_A), newv_p.reshape(1, n_p, t_p, H_A, DH_A),
            newik_p.reshape(1, n_p, t_p, D_IDX), sfin_p[None],
            newk_s.reshape(1, n_s, t_s, H_A, DH_A), newv_s.reshape(1, n_s, t_s, H_A, DH_A),
            newik_s.reshape(1, n_s, t_s, D_IDX), sfin_s[None])
```

```python
import functools
import math

import jax
import jax.numpy as jnp
import numpy as np
from jax import lax
from jax.experimental import pallas as pl
from jax.experimental.pallas import tpu as pltpu

F32 = jnp.float32
BF16 = jnp.bfloat16
I32 = jnp.int32

D_MODEL = 2048
PAGE_SIZE = 128
DH_A = 128
H_A = 8
H_IDX = 16
D_IDX = 64
TOPK_MAX = 256
DK_B = 128
DV_B = 256
H_B = 4
GATE_RANK = 16
GATE_TAU = 16.0
ROPE_THETA = 10000.0
EPS = 1e-6
NEG_BIG = -1e30
WA = H_A * DH_A
WI = H_IDX * D_IDX
WKB = H_B * DK_B
WVB = H_B * DV_B

C_QA, C_KA, C_VA, C_GA, C_QI = 0, WA, 2 * WA, 3 * WA, 4 * WA
C_QB = C_QI + WI
C_KB = C_QB + WKB
C_VB = C_KB + WKB
C_GB = C_VB + WVB
N_MAIN = C_GB + WVB
L_KI, L_WI, L_AB = 0, D_IDX, D_IDX + H_IDX
N_TAIL = 128

LANES = 128
INT_MIN = -2 ** 31
VMEM_LIMIT = 48 * 1024 * 1024
FUSED_VMEM_LIMIT = 56 * 1024 * 1024


def _prep_w_in(w_in):
    o_ki = C_QI + WI
    o_qb = o_ki + D_IDX + H_IDX
    o_ab = o_qb + 2 * WKB + 2 * WVB
    w_bf = w_in.astype(BF16)
    pad = jnp.zeros((w_in.shape[0], N_TAIL - D_IDX - H_IDX - GATE_RANK), BF16)
    tail = jnp.concatenate([w_bf[:, o_ki:o_qb], w_bf[:, o_ab:], pad], axis=1)
    return w_bf, w_bf[:, o_qb:o_ab], tail


def _cparams(n_axes):
    return pltpu.CompilerParams(dimension_semantics=("arbitrary",) * n_axes,
                                vmem_limit_bytes=VMEM_LIMIT)


def _proj_kernel(x_ref, g_ref, wa_ref, wb_ref, o_ref, h_ref, *, n_a):
    _proj_body(pl.program_id(1), x_ref, g_ref, wa_ref, wb_ref, o_ref, h_ref, n_a)


def _proj_body(j, x_ref, g_ref, wa_ref, wb_ref, o_ref, h_ref, n_a):
    @pl.when(j == 0)
    def _():
        x = x_ref[...]
        ms = jnp.mean(x * x, axis=-1, keepdims=True)
        h_ref[...] = (x * lax.rsqrt(ms + EPS) * g_ref[...]).astype(BF16)

    w = jnp.where(j < n_a, wa_ref[...], wb_ref[...])
    o_ref[...] = jnp.dot(h_ref[...], w, preferred_element_type=F32)


def _proj(x2d, g, w_a, n_cols_a, w_b, tm, tn):
    m, k = x2d.shape
    n_a = n_cols_a // tn
    if w_b is None:
        w_b, n_b = w_a, 0
    else:
        n_b = w_b.shape[1] // tn
    return pl.pallas_call(
        functools.partial(_proj_kernel, n_a=n_a),
        grid=(m // tm, n_a + n_b),
        in_specs=[pl.BlockSpec((tm, k), lambda i, j: (i, 0)),
                  pl.BlockSpec((1, k), lambda i, j: (0, 0)),
                  pl.BlockSpec((k, tn), lambda i, j: (0, jnp.minimum(j, n_a - 1))),
                  pl.BlockSpec((k, tn), lambda i, j: (0, jnp.maximum(j - n_a, 0)))],
        out_specs=pl.BlockSpec((tm, tn), lambda i, j: (i, j)),
        out_shape=jax.ShapeDtypeStruct((m, (n_a + n_b) * tn), F32),
        scratch_shapes=[pltpu.VMEM((tm, k), BF16)],
        compiler_params=_cparams(2),
        name="proj",
    )(x2d, g, w_a, w_b)


def _rope128(x, cos, sin_signed):
    return x * cos + pltpu.roll(x, 64, axis=1) * sin_signed


def _rope64(x, cos, sin_lo, sin_hi):
    return x * cos + pltpu.roll(x, 96, axis=1) * sin_lo + pltpu.roll(x, 32, axis=1) * sin_hi


def _rope_kernel(qa_ref, ka_ref, va_ref, qi_ref, tail_ref, tab_ref,
                 newk_ref, newv_ref, newik_ref, qbf_ref, kbf_ref, vbf_ref, qibf_ref,
                 kie_ref, kio_ref):
    cos_a, sin_a = tab_ref[0], tab_ref[1]
    cos_i, sin_lo, sin_hi = tab_ref[2], tab_ref[3], tab_ref[4]
    scale = (DH_A ** -0.5) * math.log2(math.e)
    for h in range(H_A):
        sl = slice(h * DH_A, (h + 1) * DH_A)
        q = _rope128(qa_ref[:, sl], cos_a, sin_a)
        qbf_ref[:, sl] = (q * scale).astype(BF16)
        k = _rope128(ka_ref[:, sl], cos_a, sin_a)
        newk_ref[:, sl] = k
        kbf_ref[:, sl] = k.astype(BF16)
    v = va_ref[...]
    newv_ref[...] = v
    ones = jnp.ones((v.shape[0], DH_A), BF16)
    for h in range(H_A):
        vbf_ref[:, 2 * h * DH_A:(2 * h + 1) * DH_A] = v[:, h * DH_A:(h + 1) * DH_A].astype(BF16)
        vbf_ref[:, (2 * h + 1) * DH_A:(2 * h + 2) * DH_A] = ones
    for j in range(WI // LANES):
        sl = slice(j * LANES, (j + 1) * LANES)
        qibf_ref[:, sl] = _rope64(qi_ref[:, sl], cos_i, sin_lo, sin_hi).astype(BF16)
    ki = _rope64(tail_ref[...], cos_i, sin_lo, sin_hi)
    newik_ref[...] = ki[:, L_KI:L_KI + D_IDX]
    lane = lax.broadcasted_iota(I32, ki.shape, 1)
    ki_lo = jnp.where(lane < D_IDX, ki, 0.0)
    kie_ref[...] = ki_lo.astype(BF16)
    kio_ref[...] = pltpu.roll(ki_lo, D_IDX, axis=1).astype(BF16)


def _rope(z, tail, tab, tm):
    m = z.shape[0]
    n_tab = tab.shape[1] // tm
    col = lambda c: pl.BlockSpec((tm, WA), lambda i, c=c: (i, c))
    row = lambda w: pl.BlockSpec((tm, w), lambda i: (i, 0))
    return pl.pallas_call(
        _rope_kernel,
        grid=(m // tm,),
        in_specs=[col(C_QA // WA), col(C_KA // WA), col(C_VA // WA), col(C_QI // WA),
                  row(N_TAIL),
                  pl.BlockSpec((5, tm, LANES), lambda i: (0, i % n_tab, 0))],
        out_specs=[row(WA), row(WA), row(D_IDX), row(WA), row(WA), row(2 * WA), row(WI),
                   row(LANES), row(LANES)],
        out_shape=[jax.ShapeDtypeStruct((m, WA), F32), jax.ShapeDtypeStruct((m, WA), F32),
                   jax.ShapeDtypeStruct((m, D_IDX), F32),
                   jax.ShapeDtypeStruct((m, WA), BF16), jax.ShapeDtypeStruct((m, WA), BF16),
                   jax.ShapeDtypeStruct((m, 2 * WA), BF16), jax.ShapeDtypeStruct((m, WI), BF16),
                   jax.ShapeDtypeStruct((m, LANES), BF16), jax.ShapeDtypeStruct((m, LANES), BF16)],
        compiler_params=_cparams(1),
        name="rope",
    )(z, z, z, z, tail, tab)


def _rope_tables(pos):
    pos = np.asarray(pos).astype(np.float64)[:, None]
    half_a, half_i = DH_A // 2, D_IDX // 2
    inv_a = ROPE_THETA ** (-np.arange(half_a, dtype=np.float64) / half_a)
    inv_i = ROPE_THETA ** (-np.arange(half_i, dtype=np.float64) / half_i)
    ang_a, ang_i = pos * inv_a, pos * inv_i
    ca, sa = np.cos(ang_a), np.sin(ang_a)
    ci, si = np.cos(ang_i), np.sin(ang_i)
    zi = np.zeros_like(si)
    return jnp.asarray(np.stack([
        np.concatenate([ca, ca], axis=1),
        np.concatenate([-sa, sa], axis=1),
        np.concatenate([ci, ci, ci, ci], axis=1),
        np.concatenate([-si, zi, -si, zi], axis=1),
        np.concatenate([zi, si, zi, si], axis=1)]).astype(np.float32))


IDX_W_SCALE = (H_IDX ** -0.5) * (D_IDX ** -0.5)


def _idx_scores(q_ref, w, kie, kio):
    nt = (((1,), (1,)), ((), ()))
    acc = jnp.zeros((q_ref.shape[0], kie.shape[0]), F32)
    for j in range(H_IDX // 2):
        qp = q_ref[:, j * LANES:(j + 1) * LANES]
        de = lax.dot_general(qp, kie, nt, preferred_element_type=F32)
        acc = acc + w[:, 2 * j:2 * j + 1] * jnp.maximum(de, 0.0)
        do = lax.dot_general(qp, kio, nt, preferred_element_type=F32)
        acc = acc + w[:, 2 * j + 1:2 * j + 2] * jnp.maximum(do, 0.0)
    return acc


def _idx_prompt_kernel(q_ref, tail_ref, kie_ref, kio_ref, o_ref, *, tq, tc, t_len):
    i = pl.program_id(1)
    w = tail_ref[:, L_WI:L_WI + H_IDX] * IDX_W_SCALE
    n_c = (i * tq + tq - 1) // tc + 1
    row = i * tq + lax.broadcasted_iota(I32, (tq, tc), 0)

    def body(c, carry):
        c0 = pl.multiple_of(c * tc, tc)
        acc = _idx_scores(q_ref, w, kie_ref[pl.ds(c0, tc), :], kio_ref[pl.ds(c0, tc), :])
        col = c0 + lax.broadcasted_iota(I32, (tq, tc), 1)
        o_ref[:, pl.ds(c0, tc)] = jnp.where(col <= row, acc, -jnp.inf)
        return carry

    lax.fori_loop(0, n_c, body, 0)

    def fill(c, carry):
        c0 = pl.multiple_of(c * tc, tc)
        o_ref[:, pl.ds(c0, tc)] = jnp.full((tq, tc), -jnp.inf, F32)
        return carry

    lax.fori_loop(n_c, t_len // tc, fill, 0)


def _idx_prompt(qi_bf, tail, kie, kio, n_batch, t_len, tq=128, tc=256):
    nq = t_len // tq
    kern = functools.partial(_idx_prompt_kernel, tq=tq, tc=tc, t_len=t_len)
    return pl.pallas_call(
        kern,
        grid=(n_batch, nq),
        in_specs=[pl.BlockSpec((tq, WI), lambda b, i: (b * nq + i, 0)),
                  pl.BlockSpec((tq, N_TAIL), lambda b, i: (b * nq + i, 0)),
                  pl.BlockSpec((t_len, LANES), lambda b, i: (b, 0)),
                  pl.BlockSpec((t_len, LANES), lambda b, i: (b, 0))],
        out_specs=pl.BlockSpec((tq, t_len), lambda b, i: (b * nq + i, 0)),
        out_shape=jax.ShapeDtypeStruct((n_batch * t_len, t_len), F32),
        compiler_params=_cparams(2),
        name="idx_prompt",
    )(qi_bf, tail, kie, kio)


SELECT_STEPS_PER_CHECK = 3
SELECT_MAX_CHECKS = 16
SELECT_ROWS = 128


def _select_body(s_ref, o_ref, *, n_half, sub, n_all, n_c, unroll, k_top):
    tc = sub * LANES
    kf = float(k_top)
    halves = [pl.ds(hh * SELECT_ROWS, SELECT_ROWS) for hh in range(n_half)]
    shape = (SELECT_ROWS, LANES)

    def over_chunks(fn, init, lo_c, hi_c):
        if unroll:
            carry = init
            for c in range(lo_c, hi_c):
                carry = fn(c * tc, carry)
            return carry
        return lax.fori_loop(lo_c, hi_c, lambda c, carry: fn(pl.multiple_of(c * tc, tc), carry), init)

    def stats(c0, carry):
        out = []
        for rows, (mn, mx, cnt) in zip(halves, carry):
            for u in range(sub):
                x = s_ref[rows, pl.ds(c0 + u * LANES, LANES)]
                valid = x > -jnp.inf
                mn = jnp.minimum(mn, jnp.where(valid, x, jnp.inf))
                mx = jnp.maximum(mx, x)
                cnt = cnt + jnp.where(valid, 1.0, 0.0)
            out.append((mn, mx, cnt))
        return tuple(out)

    st = over_chunks(stats, tuple((jnp.full(shape, jnp.inf, F32), jnp.full(shape, -jnp.inf, F32),
                                   jnp.zeros(shape, F32)) for _ in halves), 0, n_c)
    lo0 = tuple(jnp.min(mn, axis=1, keepdims=True) for mn, _, _ in st)
    hi0 = tuple(jnp.max(mx, axis=1, keepdims=True) for _, mx, _ in st)
    c0_lo = tuple(jnp.sum(cnt, axis=1, keepdims=True) for _, _, cnt in st)

    def count_ge(thrs):
        thr_b = [jnp.broadcast_to(t, shape) for t in thrs]

        def body(c0, accs):
            out = []
            for rows, tb, acc in zip(halves, thr_b, accs):
                for u in range(sub):
                    x = s_ref[rows, pl.ds(c0 + u * LANES, LANES)]
                    acc = acc + jnp.where(x >= tb, 1.0, 0.0)
                out.append(acc)
            return tuple(out)

        accs = over_chunks(body, tuple(jnp.zeros(shape, F32) for _ in halves), 0, n_c)
        return [jnp.sum(a, axis=1, keepdims=True) for a in accs]

    def unresolved(state):
        it, _, _, c_lo = state
        worst = c_lo[0]
        for c in c_lo[1:]:
            worst = jnp.maximum(worst, c)
        return jnp.logical_and(it < SELECT_MAX_CHECKS, jnp.max(worst) > kf)

    def refine(state):
        it, lo, hi, c_lo = state
        lo, hi, c_lo = list(lo), list(hi), list(c_lo)
        for _ in range(SELECT_STEPS_PER_CHECK):
            mid = [0.5 * a + 0.5 * b for a, b in zip(lo, hi)]
            cnt = count_ge(mid)
            for i in range(n_half):
                ok = cnt[i] >= kf
                lo[i] = jnp.where(ok, mid[i], lo[i])
                c_lo[i] = jnp.where(ok, cnt[i], c_lo[i])
                hi[i] = jnp.where(ok, hi[i], mid[i])
        return it + 1, tuple(lo), tuple(hi), tuple(c_lo)

    _, thr, _, _ = lax.while_loop(unresolved, refine, (jnp.int32(0), lo0, hi0, c0_lo))

    def emit(c0, carry):
        for rows, t in zip(halves, thr):
            x = s_ref[rows, pl.ds(c0, tc)]
            o_ref[rows, pl.ds(c0, tc)] = jnp.where(x >= t, 0.0, NEG_BIG).astype(o_ref.dtype)
        return carry

    over_chunks(emit, 0, 0, n_c)

    def fill(c0, carry):
        o_ref[:, pl.ds(c0, tc)] = jnp.full((n_half * SELECT_ROWS, tc), NEG_BIG, o_ref.dtype)
        return carry

    over_chunks(fill, 0, n_c, n_all)


def _select_kernel(s_ref, o_ref, *, n_half, sub, n_cols, k_top, rows_per_batch, n_groups):
    tc = sub * LANES
    n_all = n_cols // tc
    body = functools.partial(_select_body, s_ref, o_ref, n_half=n_half, sub=sub, n_all=n_all, k_top=k_top)
    if rows_per_batch:
        tr = n_half * SELECT_ROWS
        r0 = (pl.program_id(0) % (rows_per_batch // tr)) * tr
        rows_per_group = rows_per_batch // n_groups
        for g in range(n_groups):
            @pl.when(r0 // rows_per_group == g)
            def _(g=g):
                body(n_c=((g + 1) * rows_per_group - 1) // tc + 1, unroll=True)
    else:
        body(n_c=n_all, unroll=False)


def _select(scores, k_top, sub, rows_per_batch, out_dtype, n_half=1, n_groups=1):
    m, n_cols = scores.shape
    tr = n_half * SELECT_ROWS
    kern = functools.partial(_select_kernel, n_half=n_half, sub=sub, n_cols=n_cols, k_top=k_top,
                             rows_per_batch=rows_per_batch, n_groups=n_groups)
    return pl.pallas_call(
        kern,
        grid=(m // tr,),
        in_specs=[pl.BlockSpec((tr, n_cols), lambda i: (i, 0))],
        out_specs=pl.BlockSpec((tr, n_cols), lambda i: (i, 0)),
        out_shape=jax.ShapeDtypeStruct((m, n_cols), out_dtype),
        compiler_params=_cparams(1),
        name="select",
    )(scores)


def _attn_prompt_kernel(qi_ref, kj_ref, q_ref, k_ref, v_ref, b_ref, o_ref, m_ref, l_ref, acc_ref,
                        *, tq, ts):
    step = pl.program_id(1)
    i, j = qi_ref[step], kj_ref[step]
    n_rep = ts // LANES

    @pl.when(j == 0)
    def _():
        m_ref[...] = jnp.full(m_ref.shape, NEG_BIG, F32)
        l_ref[...] = jnp.zeros(l_ref.shape, F32)
        acc_ref[...] = jnp.zeros(acc_ref.shape, F32)

    def accumulate():
        bias = b_ref[...].astype(F32)
        nt = (((1,), (1,)), ((), ()))
        heads = [slice(h * DH_A, (h + 1) * DH_A) for h in range(H_A)]
        scores = [lax.dot_general(q_ref[:, sl], k_ref[:, sl], nt, preferred_element_type=F32) + bias
                  for sl in heads]
        probs, alphas = [], []
        for h, s in enumerate(scores):
            m_prev = m_ref[h]
            m_next = jnp.maximum(m_prev, jnp.max(s, axis=1, keepdims=True))
            alphas.append(jnp.exp2(m_prev - m_next))
            probs.append(jnp.exp2(s - jnp.tile(m_next, (1, n_rep))).astype(BF16))
            m_ref[h] = m_next
        for h, (sl, p, alpha) in enumerate(zip(heads, probs, alphas)):
            pv = jnp.dot(p, v_ref[:, 2 * h * DH_A:(2 * h + 2) * DH_A], preferred_element_type=F32)
            acc_ref[:, sl] = acc_ref[:, sl] * alpha + pv[:, :DH_A]
            l_ref[h] = l_ref[h] * alpha + pv[:, DH_A:]

    accumulate()

    @pl.when(j == (i * tq + tq - 1) // ts)
    def _():
        for h in range(H_A):
            sl = slice(h * DH_A, (h + 1) * DH_A)
            o_ref[:, sl] = (acc_ref[:, sl] / l_ref[h]).astype(o_ref.dtype)


def _attn_prompt(q_bf, k_bf, v_bf, bias, n_batch, t_len, tq=256, ts=1024):
    nq, ns = t_len // tq, t_len // ts
    pairs = [(i, j) for i in range(nq) for j in range((i * tq + tq - 1) // ts + 1)]
    q_idx = jnp.asarray([p[0] for p in pairs], I32)
    k_idx = jnp.asarray([p[1] for p in pairs], I32)
    kern = functools.partial(_attn_prompt_kernel, tq=tq, ts=ts)
    return pl.pallas_call(
        kern,
        grid_spec=pltpu.PrefetchScalarGridSpec(
            num_scalar_prefetch=2,
            grid=(n_batch, len(pairs)),
            in_specs=[pl.BlockSpec((tq, WA), lambda b, s, qi, kj: (b * nq + qi[s], 0)),
                      pl.BlockSpec((ts, WA), lambda b, s, qi, kj: (b * ns + kj[s], 0)),
                      pl.BlockSpec((ts, 2 * WA), lambda b, s, qi, kj: (b * ns + kj[s], 0)),
                      pl.BlockSpec((tq, ts), lambda b, s, qi, kj: (b * nq + qi[s], kj[s]))],
            out_specs=pl.BlockSpec((tq, WA), lambda b, s, qi, kj: (b * nq + qi[s], 0)),
            scratch_shapes=[pltpu.VMEM((H_A, tq, LANES), F32), pltpu.VMEM((H_A, tq, LANES), F32),
                            pltpu.VMEM((tq, WA), F32)]),
        out_shape=jax.ShapeDtypeStruct((n_batch * t_len, WA), F32),
        compiler_params=_cparams(2),
        name="attn_prompt",
    )(q_idx, k_idx, q_bf, k_bf, v_bf, bias)


def _log_sigmoid(x):
    return jnp.minimum(x, 0.0) - jnp.log1p(jnp.exp(-jnp.abs(x)))


def _bf16_split3(x):
    x1 = x.astype(BF16)
    r1 = x - x1.astype(F32)
    x2 = r1.astype(BF16)
    x3 = (r1 - x2.astype(F32)).astype(BF16)
    return x1, x2, x3


def _gla_kernel(q_ref, k_ref, v_ref, tail_ref, wg_ref, bg_ref, s0_ref, o_ref, sfin_ref,
                st_ref, cum_all, kk_all, xs_all, *, tb, chunk, sub):
    c_id = pl.program_id(1)
    n_sub = chunk // sub
    nt = (((1,), (1,)), ((), ()))
    tn = (((0,), (0,)), ((), ()))

    @pl.when(c_id == 0)
    def _():
        for h in range(H_B):
            st_ref[h] = s0_ref[h].T

    rows = lax.broadcasted_iota(I32, (chunk, LANES), 0)
    cols = lax.broadcasted_iota(I32, (chunk, LANES), 1)
    tri = (lax.broadcasted_iota(I32, (chunk, chunk), 1)
           <= lax.broadcasted_iota(I32, (chunk, chunk), 0)).astype(BF16)
    ones = jnp.ones((LANES, LANES), BF16)

    def load(ref, r0, n_rows, c0, width):
        x = ref[pl.ds(r0, n_rows), c0:c0 + width]
        if n_rows < chunk:
            x = jnp.concatenate([x, jnp.zeros((chunk - n_rows, width), x.dtype)], axis=0)
        return x

    n_rows = min(tb, chunk)
    n_chunks = max(tb // chunk, 1)
    items = [(c, h) for c in range(n_chunks) for h in range(H_B)]
    slot_of = {it: i for i, it in enumerate(items)}
    lane_hit = [(cols % sub == sl) & (cols // sub == rows // sub) & (cols < chunk)
                for sl in range(sub)]
    row_blk = rows // sub

    ab = [load(tail_ref, c * chunk, n_rows, 0, N_TAIL)[:, L_AB:L_AB + GATE_RANK].astype(BF16)
          for c in range(n_chunks)]
    wg = [wg_ref[:, h * DK_B:(h + 1) * DK_B].astype(BF16) for h in range(H_B)]
    q, k, vb, la = {}, {}, {}, {}
    for c, h in items:
        q[c, h] = load(q_ref, c * chunk, n_rows, h * DK_B, DK_B) * (DK_B ** -0.5)
        k[c, h] = load(k_ref, c * chunk, n_rows, h * DK_B, DK_B)
        vb[c, h] = load(v_ref, c * chunk, n_rows, h * DV_B, DV_B).astype(BF16)
        x = jnp.dot(ab[c], wg[h], preferred_element_type=F32) + bg_ref[:, h * DK_B:(h + 1) * DK_B]
        la[c, h] = _log_sigmoid(x) / GATE_TAU
        if n_rows < chunk:
            la[c, h] = jnp.where(rows < n_rows, la[c, h], 0.0)

    cum = {}
    for it in items:
        l1, l2, l3 = _bf16_split3(la[it])
        cum[it] = (jnp.dot(tri, l1, preferred_element_type=F32)
                   + jnp.dot(tri, l2, preferred_element_type=F32)
                   + jnp.dot(tri, l3, preferred_element_type=F32))
        cum_all[slot_of[it]] = cum[it]
        kk_all[slot_of[it]] = k[it]

    att = {}
    for it in items:
        cum_ref = cum_all.at[slot_of[it]]
        a = jnp.zeros((chunk, LANES), F32)
        for i in range(1, n_sub):
            r_i = cum_ref[i * sub - 1:i * sub, :]
            qt = (q[it] * jnp.exp(jnp.minimum(cum[it] - r_i, 0.0))).astype(BF16)
            kt = (k[it] * jnp.exp(jnp.minimum(r_i - cum[it], 0.0))).astype(BF16)
            a_i = lax.dot_general(qt, kt, nt, preferred_element_type=F32)
            if chunk < LANES:
                a_i = jnp.concatenate([a_i, jnp.zeros((chunk, LANES - chunk), F32)], axis=1)
            a = jnp.where((row_blk == i) & (cols < i * sub), a_i, a)
        att[it] = a

    for it in items:
        cum_ref, kk_ref, xs_ref = (r.at[slot_of[it]] for r in (cum_all, kk_all, xs_all))
        for sl in range(sub):
            cum_s = jnp.concatenate(
                [jnp.broadcast_to(cum_ref[i * sub + sl:i * sub + sl + 1, :], (sub, LANES))
                 for i in range(n_sub)], axis=0)
            k_s = jnp.concatenate(
                [jnp.broadcast_to(kk_ref[i * sub + sl:i * sub + sl + 1, :], (sub, LANES))
                 for i in range(n_sub)], axis=0)
            dec = jnp.where(rows % sub >= sl, jnp.exp(jnp.minimum(cum[it] - cum_s, 0.0)), 0.0)
            xs_ref[sl * chunk:(sl + 1) * chunk, :] = (q[it] * k_s * dec).astype(BF16)
    n_x = sub * chunk
    red = jnp.dot(xs_all[...].reshape(len(items) * n_x, LANES), ones, preferred_element_type=F32)

    o_intra, upd, qdec, sdec = {}, {}, {}, {}
    for it in items:
        a = att[it]
        base = slot_of[it] * n_x
        for sl in range(sub):
            a = jnp.where(lane_hit[sl], red[base + sl * chunk:base + (sl + 1) * chunk, :], a)
        o_intra[it] = jnp.dot(a[:, :chunk].astype(BF16), vb[it], preferred_element_type=F32)
        last = cum_all[slot_of[it], chunk - 1:chunk, :]
        kd = (k[it] * jnp.exp(last - cum[it])).astype(BF16)
        upd[it] = lax.dot_general(vb[it], kd, tn, preferred_element_type=F32)
        qdec[it] = (q[it] * jnp.exp(cum[it])).astype(BF16)
        sdec[it] = jnp.exp(last)

    for h in range(H_B):
        st = st_ref[h]
        for c in range(n_chunks):
            o = o_intra[c, h] + lax.dot_general(qdec[c, h], st.astype(BF16), nt,
                                                preferred_element_type=F32)
            o_ref[c * chunk:c * chunk + n_rows, h * DV_B:(h + 1) * DV_B] = o[:n_rows]
            st = st * sdec[c, h] + upd[c, h]
        st_ref[h] = st

    @pl.when(c_id == pl.num_programs(1) - 1)
    def _():
        for h in range(H_B):
            sfin_ref[h] = st_ref[h].T


def _gla(z, tail, w_gate_up, b_gate, s0, n_batch, t_len, tb, chunk, sub):
    nb = t_len // tb
    n_slots = max(tb // chunk, 1) * H_B
    kern = functools.partial(_gla_kernel, tb=tb, chunk=chunk, sub=sub)
    return pl.pallas_call(
        kern,
        grid=(n_batch, nb),
        in_specs=[pl.BlockSpec((tb, WKB), lambda b, c: (b * nb + c, C_QB // WKB)),
                  pl.BlockSpec((tb, WKB), lambda b, c: (b * nb + c, C_KB // WKB)),
                  pl.BlockSpec((tb, WVB), lambda b, c: (b * nb + c, C_VB // WVB)),
                  pl.BlockSpec((tb, N_TAIL), lambda b, c: (b * nb + c, 0)),
                  pl.BlockSpec((GATE_RANK, WKB), lambda b, c: (0, 0)),
                  pl.BlockSpec((1, WKB), lambda b, c: (0, 0)),
                  pl.BlockSpec((None, H_B, DK_B, DV_B), lambda b, c: (b, 0, 0, 0))],
        out_specs=[pl.BlockSpec((tb, WVB), lambda b, c: (b * nb + c, 0)),
                   pl.BlockSpec((None, H_B, DK_B, DV_B), lambda b, c: (b, 0, 0, 0))],
        out_shape=[jax.ShapeDtypeStruct((n_batch * t_len, WVB), F32),
                   jax.ShapeDtypeStruct((n_batch, H_B, DK_B, DV_B), F32)],
        scratch_shapes=[pltpu.VMEM((H_B, DV_B, DK_B), F32), pltpu.VMEM((n_slots, chunk, LANES), F32),
                        pltpu.VMEM((n_slots, chunk, LANES), F32),
                        pltpu.VMEM((n_slots, sub * chunk, LANES), BF16)],
        compiler_params=_cparams(2),
        name="gla",
    )(z, z, z, tail, w_gate_up, b_gate, s0)


def _silu(x):
    return x / (1.0 + jnp.exp(-x))


def _merge_kernel(att_ref, ga_ref, ob_ref, gb_ref, gn_ref, wo_ref, x_ref, nf_ref, o_ref):
    a = (att_ref[...] * _silu(ga_ref[...])).astype(BF16)
    y = jnp.dot(a, wo_ref[:WA, :], preferred_element_type=F32)
    gn = gn_ref[...]
    for h in range(H_B):
        sl = slice(h * DV_B, (h + 1) * DV_B)
        ob = ob_ref[:, sl]
        bn = ob * lax.rsqrt(jnp.mean(ob * ob, axis=-1, keepdims=True) + EPS) * gn
        bp = (bn * _silu(gb_ref[:, sl])).astype(BF16)
        y = y + jnp.dot(bp, wo_ref[WA + h * DV_B:WA + (h + 1) * DV_B, :], preferred_element_type=F32)
    xo = x_ref[...] + y
    o_ref[...] = xo * lax.rsqrt(jnp.mean(xo * xo, axis=-1, keepdims=True) + EPS) * nf_ref[...]


def _merge(att, z, ob, gla_norm, w_out_bf, x2d, norm_f, tm):
    m = x2d.shape[0]
    return pl.pallas_call(
        _merge_kernel,
        grid=(m // tm,),
        in_specs=[pl.BlockSpec((tm, WA), lambda i: (i, 0)),
                  pl.BlockSpec((tm, WA), lambda i: (i, C_GA // WA)),
                  pl.BlockSpec((tm, WVB), lambda i: (i, 0)),
                  pl.BlockSpec((tm, WVB), lambda i: (i, C_GB // WVB)),
                  pl.BlockSpec((1, DV_B), lambda i: (0, 0)),
                  pl.BlockSpec((WA + WVB, D_MODEL), lambda i: (0, 0)),
                  pl.BlockSpec((tm, D_MODEL), lambda i: (i, 0)),
                  pl.BlockSpec((1, D_MODEL), lambda i: (0, 0))],
        out_specs=pl.BlockSpec((tm, D_MODEL), lambda i: (i, 0)),
        out_shape=jax.ShapeDtypeStruct((m, D_MODEL), F32),
        compiler_params=_cparams(1),
        name="merge",
    )(att, z, ob, z, gla_norm, w_out_bf, x2d, norm_f)


PAGES_PER_STEP = 32


def _idx_pages_kernel(pt_ref, q_ref, w_ref, *refs, n_pages, t_dec, causal):
    page_refs, o_ref = refs[:n_pages], refs[n_pages]
    q = q_ref[...]
    w = w_ref[...]
    for p in range(n_pages):
        kp = page_refs[p][...].astype(BF16)
        d = jnp.dot(q, kp, preferred_element_type=F32)
        r = w * jnp.maximum(d, 0.0)
        acc = r[0:t_dec]
        for h in range(1, H_IDX):
            acc = acc + r[h * t_dec:(h + 1) * t_dec]
        if causal:
            t_i = lax.broadcasted_iota(I32, acc.shape, 0)
            s_i = lax.broadcasted_iota(I32, acc.shape, 1)
            acc = jnp.where(s_i <= t_i, acc, -jnp.inf)
        o_ref[:, p * PAGE_SIZE:(p + 1) * PAGE_SIZE] = acc


def _idx_sample_past(page_table, q_stk, w_stk, pool_ik, n_seq, t_dec, n_pages):
    pps = min(PAGES_PER_STEP, n_pages)
    n_g = n_pages // pps
    rows = H_IDX * t_dec
    kern = functools.partial(_idx_pages_kernel, n_pages=pps, t_dec=t_dec, causal=False)
    page_spec = lambda r: pl.BlockSpec(
        (None, D_IDX, PAGE_SIZE), lambda b, g, pt, r=r: (pt[b * n_pages + g * pps + r], 0, 0))
    return pl.pallas_call(
        kern,
        grid_spec=pltpu.PrefetchScalarGridSpec(
            num_scalar_prefetch=1,
            grid=(n_seq, n_g),
            in_specs=[pl.BlockSpec((rows, D_IDX), lambda b, g, pt: (b, 0)),
                      pl.BlockSpec((rows, 1), lambda b, g, pt: (b, 0))]
                     + [page_spec(r) for r in range(pps)],
            out_specs=pl.BlockSpec((t_dec, pps * PAGE_SIZE), lambda b, g, pt: (b, g))),
        out_shape=jax.ShapeDtypeStruct((n_seq * t_dec, n_pages * PAGE_SIZE), F32),
        compiler_params=_cparams(2),
        name="idx_sample_past",
    )(page_table.reshape(-1), q_stk, w_stk, *([pool_ik] * pps))


def _idx_sample_new(q_stk, w_stk, ki_new_pad, n_seq, t_dec):
    rows = H_IDX * t_dec
    kern = functools.partial(_idx_pages_kernel, None, n_pages=1, t_dec=t_dec, causal=True)
    return pl.pallas_call(
        kern,
        grid=(n_seq,),
        in_specs=[pl.BlockSpec((rows, D_IDX), lambda b: (b, 0)),
                  pl.BlockSpec((rows, 1), lambda b: (b, 0)),
                  pl.BlockSpec((None, D_IDX, PAGE_SIZE), lambda b: (b, 0, 0))],
        out_specs=pl.BlockSpec((t_dec, PAGE_SIZE), lambda b: (b, 0)),
        out_shape=jax.ShapeDtypeStruct((n_seq * t_dec, PAGE_SIZE), F32),
        compiler_params=_cparams(1),
        name="idx_sample_new",
    )(q_stk, w_stk, ki_new_pad)


ATTN_PAGES_PER_STEP = 8


def _attn_sample_kernel(pt_ref, q_ref, sel_ref, selnew_ref, e_ref, knew_ref, vnew_ref, *refs,
                        n_pages, t_dec):
    k_refs, v_refs = refs[:n_pages], refs[n_pages:2 * n_pages]
    o_ref, m_ref, l_ref, acc_ref = refs[2 * n_pages:]
    _attn_sample_body(pl.program_id(1), pl.num_programs(1), q_ref, sel_ref, selnew_ref, e_ref,
                      knew_ref, vnew_ref, k_refs, v_refs, o_ref, m_ref, l_ref, acc_ref, t_dec)


def _attn_sample_body(g, n_g, q_ref, sel_ref, selnew_ref, e_ref, knew_ref, vnew_ref, k_refs, v_refs,
                      o_ref, m_ref, l_ref, acc_ref, t_dec):
    n_pages = len(k_refs)
    rows = H_A * t_dec
    nt = (((1,), (1,)), ((), ()))
    q = q_ref[...]
    n_col = PAGE_SIZE * H_A
    head_hot = (lax.broadcasted_iota(I32, (rows, LANES), 0) // t_dec
                == lax.broadcasted_iota(I32, (rows, LANES), 1)).astype(BF16)

    def update(sels, k_pages, v_pages):
        lhs = jnp.concatenate(
            [jnp.concatenate([jnp.tile(sel, (H_A, 1)).astype(BF16), head_hot], axis=1) for sel in sels],
            axis=0)
        bias = jnp.dot(lhs, e_ref[...], preferred_element_type=F32)
        scores = []
        m_next = m_ref[...]
        for i, k_page in enumerate(k_pages):
            s = lax.dot_general(q, k_page[...].astype(BF16), nt, preferred_element_type=F32)
            s = s + bias[i * rows:(i + 1) * rows]
            m_next = jnp.maximum(m_next, jnp.max(s, axis=1, keepdims=True))
            scores.append(s)
        alpha = jnp.exp2(m_ref[...] - m_next)
        m_rep = jnp.tile(m_next, (1, n_col // LANES))
        l_new = alpha * l_ref[...]
        acc = acc_ref[...] * alpha
        for s, v_page in zip(scores, v_pages):
            p = jnp.exp2(s - m_rep)
            l_new = l_new + jnp.sum(p, axis=1, keepdims=True)
            acc = acc + jnp.dot(p.astype(BF16), v_page[...].astype(BF16), preferred_element_type=F32)
        m_ref[...] = m_next
        l_ref[...] = l_new
        acc_ref[...] = acc

    @pl.when(g == 0)
    def _():
        m_ref[...] = jnp.full(m_ref.shape, NEG_BIG, F32)
        l_ref[...] = jnp.zeros(l_ref.shape, F32)
        acc_ref[...] = jnp.zeros(acc_ref.shape, F32)
        update([selnew_ref[...]], [knew_ref], [vnew_ref])

    update([sel_ref[:, p_i * PAGE_SIZE:(p_i + 1) * PAGE_SIZE] for p_i in range(n_pages)],
           k_refs, v_refs)

    @pl.when(g == n_g - 1)
    def _():
        o_ref[...] = acc_ref[...] / l_ref[...]


def _proj_attn_kernel(pt_ref, x_ref, g_ref, wa_ref, wb_ref, q_ref, sel_ref, selnew_ref, e_ref,
                      knew_ref, vnew_ref, *refs, n_pages, t_dec, n_a, n_j, n_g):
    k_refs, v_refs = refs[:n_pages], refs[n_pages:2 * n_pages]
    z_ref, att_ref, h_ref, m_ref, l_ref, acc_ref = refs[2 * n_pages:]
    step = pl.program_id(0)
    _proj_body(step % n_j, x_ref, g_ref, wa_ref, wb_ref, z_ref, h_ref, n_a)
    _attn_sample_body(step % n_g, n_g, q_ref, sel_ref, selnew_ref, e_ref, knew_ref, vnew_ref,
                      k_refs, v_refs, att_ref, m_ref, l_ref, acc_ref, t_dec)


def _proj_attn(x2d, g, w_a, n_cols_a, w_b, tm, tn,
               page_table, q_stk, sel_past, sel_new, expand, knew_pad, vnew_pad, pool_k, pool_v,
               n_seq, t_dec, n_pages):
    m, k = x2d.shape
    n_a, n_b = n_cols_a // tn, w_b.shape[1] // tn
    n_j = n_a + n_b
    pps = ATTN_PAGES_PER_STEP
    n_g = n_pages // pps
    n_steps = (m // tm) * n_j
    assert n_steps == n_seq * n_g
    rows = H_A * t_dec
    n_col = PAGE_SIZE * H_A
    kern = functools.partial(_proj_attn_kernel, n_pages=pps, t_dec=t_dec, n_a=n_a, n_j=n_j, n_g=n_g)
    page_spec = lambda r: pl.BlockSpec(
        (None, n_col, DH_A), lambda s, pt, r=r: (pt[(s // n_g) * n_pages + (s % n_g) * pps + r], 0, 0))
    return pl.pallas_call(
        kern,
        grid_spec=pltpu.PrefetchScalarGridSpec(
            num_scalar_prefetch=1,
            grid=(n_steps,),
            in_specs=[pl.BlockSpec((tm, k), lambda s, pt: (s // n_j, 0)),
                      pl.BlockSpec((1, k), lambda s, pt: (0, 0)),
                      pl.BlockSpec((k, tn), lambda s, pt: (0, jnp.minimum(s % n_j, n_a - 1))),
                      pl.BlockSpec((k, tn), lambda s, pt: (0, jnp.maximum(s % n_j - n_a, 0))),
                      pl.BlockSpec((rows, DH_A), lambda s, pt: (s // n_g, 0)),
                      pl.BlockSpec((t_dec, pps * PAGE_SIZE), lambda s, pt: (s // n_g, s % n_g)),
                      pl.BlockSpec((t_dec, PAGE_SIZE), lambda s, pt: (s // n_g, 0)),
                      pl.BlockSpec((2 * PAGE_SIZE, n_col), lambda s, pt: (0, 0)),
                      pl.BlockSpec((None, n_col, DH_A), lambda s, pt: (s // n_g, 0, 0)),
                      pl.BlockSpec((None, n_col, DH_A), lambda s, pt: (s // n_g, 0, 0))]
                     + [page_spec(r) for r in range(pps)] * 2,
            out_specs=[pl.BlockSpec((tm, tn), lambda s, pt: (s // n_j, s % n_j)),
                       pl.BlockSpec((rows, DH_A), lambda s, pt: (s // n_g, 0))],
            scratch_shapes=[pltpu.VMEM((tm, k), BF16),
                            pltpu.VMEM((rows, LANES), F32), pltpu.VMEM((rows, LANES), F32),
                            pltpu.VMEM((rows, DH_A), F32)]),
        out_shape=[jax.ShapeDtypeStruct((m, n_j * tn), F32),
                   jax.ShapeDtypeStruct((n_seq * rows, DH_A), F32)],
        compiler_params=pltpu.CompilerParams(dimension_semantics=("arbitrary",),
                                             vmem_limit_bytes=FUSED_VMEM_LIMIT),
        name="proj_attn",
    )(page_table.reshape(-1), x2d, g, w_a, w_b, q_stk, sel_past, sel_new, expand, knew_pad, vnew_pad,
      *([pool_k] * pps), *([pool_v] * pps))


def _attn_sample(page_table, q_stk, sel_past, sel_new, expand, knew_pad, vnew_pad, pool_k, pool_v,
                 n_seq, t_dec, n_pages):
    pps = ATTN_PAGES_PER_STEP
    n_g = n_pages // pps
    rows = H_A * t_dec
    n_col = PAGE_SIZE * H_A
    kern = functools.partial(_attn_sample_kernel, n_pages=pps, t_dec=t_dec)
    page_spec = lambda r: pl.BlockSpec(
        (None, n_col, DH_A), lambda b, g, pt, r=r: (pt[b * n_pages + g * pps + r], 0, 0))
    return pl.pallas_call(
        kern,
        grid_spec=pltpu.PrefetchScalarGridSpec(
            num_scalar_prefetch=1,
            grid=(n_seq, n_g),
            in_specs=[pl.BlockSpec((rows, DH_A), lambda b, g, pt: (b, 0)),
                      pl.BlockSpec((t_dec, pps * PAGE_SIZE), lambda b, g, pt: (b, g)),
                      pl.BlockSpec((t_dec, PAGE_SIZE), lambda b, g, pt: (b, 0)),
                      pl.BlockSpec((2 * PAGE_SIZE, n_col), lambda b, g, pt: (0, 0)),
                      pl.BlockSpec((None, n_col, DH_A), lambda b, g, pt: (b, 0, 0)),
                      pl.BlockSpec((None, n_col, DH_A), lambda b, g, pt: (b, 0, 0))]
                     + [page_spec(r) for r in range(pps)] * 2,
            out_specs=pl.BlockSpec((rows, DH_A), lambda b, g, pt: (b, 0)),
            scratch_shapes=[pltpu.VMEM((rows, LANES), F32), pltpu.VMEM((rows, LANES), F32),
                            pltpu.VMEM((rows, DH_A), F32)]),
        out_shape=jax.ShapeDtypeStruct((n_seq * rows, DH_A), F32),
        compiler_params=_cparams(2),
        name="attn_sample",
    )(page_table.reshape(-1), q_stk, sel_past, sel_new, expand, knew_pad, vnew_pad,
      *([pool_k] * pps), *([pool_v] * pps))


def _group_common(x2d, norm_in, w_parts, pos, tm, tn, tr, z=None):
    w_full, w_gla, w_tail = w_parts
    if z is None:
        z = _proj(x2d, norm_in, w_full, C_QB, w_gla, tm, tn)
    tail = _proj(x2d, norm_in, w_tail, N_TAIL, None, tm, N_TAIL)
    tab = _rope_tables(pos)
    return z, tail, _rope(z, tail, tab, tr)


def kernel(x_prompt, x_sample, cache_k, cache_v, cache_idx_k, state_gla, page_table,
           norm_in, w_in, w_gate_up, b_gate, gla_norm, w_out, norm_f):
    n_p, t_p, _ = x_prompt.shape
    n_s, t_s, _ = x_sample.shape
    n_pages = page_table.shape[1]
    past = n_pages * PAGE_SIZE
    assert cache_k.shape[0] == 1, "single layer"

    w_parts = _prep_w_in(w_in[0])
    w_out_bf = w_out[0].astype(BF16)
    g_in, g_gla, g_f = norm_in[0][None], gla_norm[0][None], norm_f[None]
    wg, bg = w_gate_up[0], b_gate[0][None]

    m_s = n_s * t_s
    pos_s = np.tile(past + np.arange(t_s), n_s)
    xs2d = x_sample.reshape(m_s, D_MODEL)
    zs, tails, (newk_s, newv_s, newik_s, q_s, k_s, v_s, qi_s, kie_s, kio_s) = _group_common(
        xs2d, g_in, w_parts, pos_s, m_s, 1024, m_s)
    qi_stk = qi_s.reshape(n_s, t_s, H_IDX, D_IDX).transpose(0, 2, 1, 3).reshape(n_s * H_IDX * t_s, D_IDX)
    w_stk = (tails[:, L_WI:L_WI + H_IDX] * IDX_W_SCALE).reshape(n_s, t_s, H_IDX).transpose(0, 2, 1)
    w_stk = w_stk.reshape(n_s * H_IDX * t_s, 1)
    q_stk = q_s.reshape(n_s, t_s, H_A, DH_A).transpose(0, 2, 1, 3).reshape(n_s * H_A * t_s, DH_A)
    pad_rows = lambda a, n: jnp.pad(a, ((0, 0), (0, n - a.shape[1]), (0, 0)))
    ki_new_pad = jnp.pad(newik_s.reshape(n_s, t_s, D_IDX).transpose(0, 2, 1),
                         ((0, 0), (0, 0), (0, PAGE_SIZE - t_s)))
    knew_pad = pad_rows(newk_s.reshape(n_s, t_s * H_A, DH_A), PAGE_SIZE * H_A)
    vnew_pad = pad_rows(newv_s.reshape(n_s, t_s * H_A, DH_A), PAGE_SIZE * H_A)
    pool_ik = jnp.swapaxes(cache_idx_k[0], 1, 2)
    pool_k = cache_k[0].reshape(-1, PAGE_SIZE * H_A, DH_A)
    pool_v = cache_v[0].reshape(-1, PAGE_SIZE * H_A, DH_A)

    sc_past = _idx_sample_past(page_table, qi_stk, w_stk, pool_ik, n_s, t_s, n_pages)
    sc_new = _idx_sample_new(qi_stk, w_stk, ki_new_pad, n_s, t_s)
    sc_s = jnp.concatenate([sc_past, sc_new], axis=1)
    k_top_s = min(TOPK_MAX, (past + t_s) // 4)
    n_tiles_s = sc_s.shape[1] // LANES
    sub_s = max(d for d in range(1, 6) if n_tiles_s % d == 0)
    bias_s = _select(sc_s, k_top_s, sub_s, 0, F32)
    col = jnp.arange(PAGE_SIZE * H_A)[None, :]
    row = jnp.arange(2 * PAGE_SIZE)[:, None]
    expand = jnp.where(row < PAGE_SIZE, (col // H_A == row).astype(F32),
                       jnp.where((row < PAGE_SIZE + H_A) & (col % H_A != row - PAGE_SIZE), NEG_BIG, 0.0)
                       ).astype(BF16)
    attn_args = (page_table, q_stk, bias_s[:, :past], bias_s[:, past:], expand,
                 knew_pad, vnew_pad, pool_k, pool_v, n_s, t_s, n_pages)

    xp2d = x_prompt.reshape(n_p * t_p, D_MODEL)
    tm_p, tn_p = 512, 512
    fused_steps = (xp2d.shape[0] // tm_p) * (N_MAIN // tn_p)
    if fused_steps == n_s * (n_pages // ATTN_PAGES_PER_STEP):
        zp, att_stk = _proj_attn(xp2d, g_in, w_parts[0], C_QB, w_parts[1], tm_p, tn_p, *attn_args)
    else:
        zp, att_stk = None, _attn_sample(*attn_args)
    att_s = att_stk.reshape(n_s, H_A, t_s, DH_A).transpose(0, 2, 1, 3).reshape(m_s, WA)

    zp, tailp, (newk_p, newv_p, newik_p, q_p, k_p, v_p, qi_p, kie_p, kio_p) = _group_common(
        xp2d, g_in, w_parts, np.arange(t_p), 1024, 512, 256, z=zp)
    sc_p = _idx_prompt(qi_p, tailp, kie_p, kio_p, n_p, t_p)
    n_groups_p = max(1, min(4, t_p // 1024))
    bias_p = _select(sc_p, min(TOPK_MAX, t_p // 4), 4, t_p, BF16, n_half=2, n_groups=n_groups_p)
    att_p = _attn_prompt(q_p, k_p, v_p, bias_p, n_p, t_p)
    s0_p = jnp.zeros((n_p, H_B, DK_B, DV_B), F32)
    ob_p, sfin_p = _gla(zp, tailp, wg, bg, s0_p, n_p, t_p, 256, 64, 16)
    y_p = _merge(att_p, zp, ob_p, g_gla, w_out_bf, xp2d, g_f, 256)

    ob_s, sfin_s = _gla(zs, tails, wg, bg, state_gla[0], n_s, t_s, t_s, 16, 16)
    y_s = _merge(att_s, zs, ob_s, g_gla, w_out_bf, xs2d, g_f, m_s)

    return (y_p.reshape(n_p, t_p, D_MODEL), y_s.reshape(n_s, t_s, D_MODEL),
            newk_p.reshape(1, n_p, t_p, H_A, DH_A), newv_p.reshape(1, n_p, t_p, H_A, DH_A),
            newik_p.reshape(1, n_p, t_p, D_IDX), sfin_p[None],
            newk_s.reshape(1, n_s, t_s, H_A, DH_A), newv_s.reshape(1, n_s, t_s, H_A, DH_A),
            newik_s.reshape(1, n_s, t_s, D_IDX), sfin_s[None])
```
